```python
import jax, jax.numpy as jnp
from jax import lax
import numpy as np

D_MODEL = 2048
BATCH = 2
SEQ = 8192
DEPTH = 2

HEAD_DIM = 128
BLOCK = 128
ROPE_THETA = 10000.0
EPS = 1e-6
NEG_INF = -1e30

SB_HEADS = 4
SB_WIDTH = SB_HEADS * HEAD_DIM
DIL_PATTERNS = ((128, 1), (512, 4), (2048, 16))
DIL_HEADS_PER_GROUP = 2
DIL_HEADS = DIL_HEADS_PER_GROUP * len(DIL_PATTERNS)
DIL_WIDTH = DIL_HEADS * HEAD_DIM
DIL_OUT_WIDTH = DIL_HEADS_PER_GROUP * HEAD_DIM
MLA_HEADS = 6
MLA_NOPE = 128
MLA_ROPE = 64
MLA_V = 128
MLA_QK = MLA_NOPE + MLA_ROPE
MLA_Q_RANK = 512
MLA_KV_RANK = 256
N_BRANCH = 3
IN_SPLITS = (SB_WIDTH, SB_WIDTH, SB_WIDTH, DIL_WIDTH, DIL_WIDTH, DIL_WIDTH,
             MLA_Q_RANK, MLA_KV_RANK, MLA_ROPE, N_BRANCH * D_MODEL)
IN_COLS = 3 * SB_WIDTH + 3 * DIL_WIDTH + MLA_Q_RANK + MLA_KV_RANK + MLA_ROPE + N_BRANCH * D_MODEL
PEER_HEADS = 8
PEER_NKEYS = 128
PEER_EXPERTS = PEER_NKEYS * PEER_NKEYS
PEER_QDIM = 256
PEER_HALF = PEER_QDIM // 2
PEER_TOPK = 16
PEER_CHUNK = 128

kernel_name = "hybrid_sb_dilated_mla_peer"


def rmsnorm(x, g):
    xf = x.astype(jnp.float32)
    y = xf * lax.rsqrt(jnp.mean(xf * xf, axis=-1, keepdims=True) + EPS)
    return (y * g.astype(jnp.float32)).astype(x.dtype)


def rope(x, positions):
    half = x.shape[-1] // 2
    inv = ROPE_THETA ** (-jnp.arange(half, dtype=jnp.float32) / half)
    ang = positions.astype(jnp.float32)[..., None] * inv
    cos = jnp.cos(ang)[:, :, None, :]
    sin = jnp.sin(ang)[:, :, None, :]
    xf = x.astype(jnp.float32)
    x1, x2 = xf[..., :half], xf[..., half:]
    return jnp.concatenate([x1 * cos - x2 * sin, x2 * cos + x1 * sin], axis=-1).astype(x.dtype)


def _split(p, sizes):
    offs, acc = [], 0
    for s_ in sizes[:-1]:
        acc += s_
        offs.append(acc)
    return jnp.split(p, offs, axis=-1)


def to_blocks(a):
    B, S, H, d = a.shape
    return a.reshape(B, S // BLOCK, BLOCK, H, d).transpose(1, 0, 2, 3, 4)


def from_blocks(a):
    nb, B, blk, H, d = a.shape
    return a.transpose(1, 0, 2, 3, 4).reshape(B, nb * blk, H, d)


def stick_breaking_attention(q, k, v):
    B, S, H, dh = q.shape
    scale = dh ** -0.5
    kpos = jnp.arange(S)

    def one_block(args):
        qb, i = args
        qpos = i * BLOCK + jnp.arange(BLOCK)
        z = jnp.einsum('bqhd,bkhd->bhqk', qb, k, preferred_element_type=jnp.float32) * scale
        mask = kpos[None, :] < qpos[:, None]
        log_beta = jax.nn.log_sigmoid(z)
        log_keep = jnp.where(mask, jax.nn.log_sigmoid(-z), 0.0)
        after = lax.cumsum(log_keep, axis=3, reverse=True) - log_keep
        w = jnp.where(mask, jnp.exp(log_beta + after), 0.0)
        return jnp.einsum('bhqk,bkhd->bqhd', w.astype(v.dtype), v)

    out = lax.map(one_block, (to_blocks(q), jnp.arange(S // BLOCK)))
    return from_blocks(out)


def causal_softmax_attention(q, k, v, scale):
    S = q.shape[1]
    kpos = jnp.arange(S)

    def one_block(args):
        qb, i = args
        qpos = i * BLOCK + jnp.arange(BLOCK)
        s = jnp.einsum('bqhd,bkhd->bhqk', qb, k, preferred_element_type=jnp.float32) * scale
        s = jnp.where(kpos[None, :] <= qpos[:, None], s, NEG_INF)
        p = jax.nn.softmax(s, axis=-1)
        return jnp.einsum('bhqk,bkhd->bqhd', p.astype(v.dtype), v)

    out = lax.map(one_block, (to_blocks(q), jnp.arange(S // BLOCK)))
    return from_blocks(out)


def dilated_group_attention(q, k, v, window, dilation):
    B, S, H, dh = q.shape
    n_back = window // dilation
    assert n_back <= BLOCK
    L = S // dilation
    nb = -(-L // BLOCK)
    Lp = nb * BLOCK

    def by_residue(a):
        a = a.reshape(B, L, dilation, H, dh).transpose(0, 2, 1, 3, 4)
        a = jnp.pad(a, ((0, 0), (0, 0), (0, Lp - L), (0, 0), (0, 0)))
        return a.reshape(B, dilation, nb, BLOCK, H, dh)

    def with_prev(a):
        prev = jnp.pad(a, ((0, 0), (0, 0), (1, 0), (0, 0), (0, 0), (0, 0)))[:, :, :-1]
        return jnp.concatenate([prev, a], axis=3)

    qr = by_residue(q)
    kw = with_prev(by_residue(k))
    vw = with_prev(by_residue(v))
    s = jnp.einsum('brnqhd,brnkhd->brnhqk', qr, kw, preferred_element_type=jnp.float32) * dh ** -0.5
    qi = jnp.arange(BLOCK)[:, None]
    kj = jnp.arange(2 * BLOCK)[None, :]
    dist = BLOCK + qi - kj
    band = (dist >= 0) & (dist <= n_back)
    valid = (jnp.arange(nb)[:, None, None] > 0) | (kj[None] >= BLOCK)
    mask = band[None] & valid
    s = jnp.where(mask[:, None], s, NEG_INF)
    m = jnp.max(s, axis=-1, keepdims=True)
    e = jnp.exp(s - m)
    den = jnp.sum(e, axis=-1, keepdims=True)
    o = jnp.einsum('brnhqk,brnkhd->brnqhd', (e / den).astype(v.dtype), vw)
    lse = (m + jnp.log(den))[..., 0]
    o = o.reshape(B, dilation, Lp, H, dh)[:, :, :L].transpose(0, 2, 1, 3, 4).reshape(B, S, H, dh)
    lse = lse.transpose(0, 1, 2, 4, 3).reshape(B, dilation, Lp, H)[:, :, :L]
    lse = lse.transpose(0, 2, 1, 3).reshape(B, S, H)
    return o, lse


def mla_attention(c_q, c_kv, k_rope, positions, cq_norm, ckv_norm, w_uq, w_ukv, q_norm, k_norm):
    B, S, _ = c_q.shape
    q = (rmsnorm(c_q, cq_norm) @ w_uq).reshape(B, S, MLA_HEADS, MLA_QK)
    kv = (rmsnorm(c_kv, ckv_norm) @ w_ukv).reshape(B, S, MLA_HEADS, MLA_NOPE + MLA_V)
    k_nope, v = kv[..., :MLA_NOPE], kv[..., MLA_NOPE:]
    k_pe = jnp.broadcast_to(k_rope[:, :, None, :], (B, S, MLA_HEADS, MLA_ROPE))
    k = jnp.concatenate([k_nope, k_pe], axis=-1)
    q = rmsnorm(q, q_norm)
    k = rmsnorm(k, k_norm)
    q = jnp.concatenate([q[..., :MLA_NOPE], rope(q[..., MLA_NOPE:], positions)], axis=-1)
    k = jnp.concatenate([k[..., :MLA_NOPE], rope(k[..., MLA_NOPE:], positions)], axis=-1)
    return causal_softmax_attention(q, k, v, MLA_QK ** -0.5)


def peer_ffn(u, w_query, sub_keys, expert_u, expert_v):
    B, S, D = u.shape
    T = B * S
    H, K = PEER_HEADS, PEER_TOPK
    ut = u.reshape(T, D)
    q = (ut @ w_query).reshape(T, H, 2, PEER_HALF)
    s = jnp.einsum('thpc,pnc->thpn', q, sub_keys, preferred_element_type=jnp.float32)
    sv, si = lax.top_k(s, K)
    cand = sv[:, :, 0, :, None] + sv[:, :, 1, None, :]
    cand_idx = si[:, :, 0, :, None] * PEER_NKEYS + si[:, :, 1, None, :]
    best, sel = lax.top_k(cand.reshape(T, H, K * K), K)
    idx = jnp.take_along_axis(cand_idx.reshape(T, H, K * K), sel, axis=-1).reshape(T, H * K)
    gate = jax.nn.softmax(best, axis=-1).reshape(T, H * K)
    nc = T // PEER_CHUNK

    def chunk(args):
        xc, ic, gc = args
        a = jnp.einsum('td,tkd->tk', xc, expert_u[ic])
        hidden = jax.nn.gelu(a, approximate=False) * gc.astype(a.dtype)
        return jnp.einsum('tk,tkd->td', hidden, expert_v[ic])

    y = lax.map(chunk, (ut.reshape(nc, PEER_CHUNK, D), idx.reshape(nc, PEER_CHUNK, H * K),
                        gate.reshape(nc, PEER_CHUNK, H * K)))
    return y.reshape(B, S, D)


def hybrid_layer(x, positions, norm_mix, w_in, dil_q_norm, dil_k_norm, mla_cq_norm, mla_ckv_norm,
                 mla_w_uq, mla_w_ukv, mla_q_norm, mla_k_norm, w_branch_sb, w_branch_dil,
                 w_branch_mla, w_out, norm_ffn, peer_w_query, peer_sub_keys, peer_u, peer_v):
    B, S, D = x.shape
    u = rmsnorm(x, norm_mix)
    proj = jnp.einsum('bsd,dc->bsc', u, w_in)
    (sb_q, sb_k, sb_v, d_q, d_k, d_v, c_q, c_kv, k_rope, gate_logits) = _split(proj, IN_SPLITS)

    def heads(a, n):
        return a.reshape(B, S, n, HEAD_DIM)

    o_sb = stick_breaking_attention(heads(sb_q, SB_HEADS), heads(sb_k, SB_HEADS),
                                    heads(sb_v, SB_HEADS)).reshape(B, S, SB_WIDTH)

    dq = rope(rmsnorm(heads(d_q, DIL_HEADS), dil_q_norm), positions)
    dk = rope(rmsnorm(heads(d_k, DIL_HEADS), dil_k_norm), positions)
    dv = heads(d_v, DIL_HEADS)
    outs, lses = [], []
    for g, (window, dilation) in enumerate(DIL_PATTERNS):
        sl = slice(g * DIL_HEADS_PER_GROUP, (g + 1) * DIL_HEADS_PER_GROUP)
        o_g, lse_g = dilated_group_attention(dq[:, :, sl], dk[:, :, sl], dv[:, :, sl], window, dilation)
        outs.append(o_g)
        lses.append(lse_g)
    alpha = jax.nn.softmax(jnp.stack(lses), axis=0)
    o_dil = jnp.einsum('gbsh,gbshd->bshd', alpha.astype(x.dtype), jnp.stack(outs)).reshape(B, S, DIL_OUT_WIDTH)

    o_mla = mla_attention(c_q, c_kv, k_rope, positions, mla_cq_norm, mla_ckv_norm, mla_w_uq, mla_w_ukv,
                          mla_q_norm, mla_k_norm).reshape(B, S, MLA_HEADS * MLA_V)

    gates = jax.nn.sigmoid(gate_logits.reshape(B, S, N_BRANCH, D))
    merged = (gates[:, :, 0] * (o_sb @ w_branch_sb)
              + gates[:, :, 1] * (o_dil @ w_branch_dil)
              + gates[:, :, 2] * (o_mla @ w_branch_mla))
    h = x + merged @ w_out

    return h + peer_ffn(rmsnorm(h, norm_ffn), peer_w_query, peer_sub_keys, peer_u, peer_v)


def setup_inputs(seed: int = 0) -> dict:
    key = jax.random.key(seed)
    ks = jax.random.split(key, 24)
    f32 = jnp.float32

    def w(k, shape, fan_in):
        return jax.random.normal(k, shape, f32) * (fan_in ** -0.5)

    def gain(k, n):
        return 1.0 + 0.02 * jax.random.normal(k, (DEPTH, n), f32)

    return {
        "x": jax.random.normal(ks[0], (BATCH, SEQ, D_MODEL), f32),
        "positions": jnp.broadcast_to(jnp.arange(SEQ, dtype=jnp.int32), (BATCH, SEQ)),
        "norm_mix": gain(ks[1], D_MODEL),
        "w_in": w(ks[2], (DEPTH, D_MODEL, IN_COLS), D_MODEL),
        "dil_q_norm": gain(ks[3], HEAD_DIM),
        "dil_k_norm": gain(ks[4], HEAD_DIM),
        "mla_cq_norm": gain(ks[5], MLA_Q_RANK),
        "mla_ckv_norm": gain(ks[6], MLA_KV_RANK),
        "mla_w_uq": w(ks[7], (DEPTH, MLA_Q_RANK, MLA_HEADS * MLA_QK), MLA_Q_RANK),
        "mla_w_ukv": w(ks[8], (DEPTH, MLA_KV_RANK, MLA_HEADS * (MLA_NOPE + MLA_V)), MLA_KV_RANK),
        "mla_q_norm": gain(ks[9], MLA_QK),
        "mla_k_norm": gain(ks[10], MLA_QK),
        "w_branch_sb": w(ks[11], (DEPTH, SB_WIDTH, D_MODEL), SB_WIDTH),
        "w_branch_dil": w(ks[12], (DEPTH, DIL_OUT_WIDTH, D_MODEL), DIL_OUT_WIDTH),
        "w_branch_mla": w(ks[13], (DEPTH, MLA_HEADS * MLA_V, D_MODEL), MLA_HEADS * MLA_V),
        "w_out": w(ks[14], (DEPTH, D_MODEL, D_MODEL), D_MODEL),
        "norm_ffn": gain(ks[15], D_MODEL),
        "peer_w_query": w(ks[16], (DEPTH, D_MODEL, PEER_HEADS * PEER_QDIM), D_MODEL),
        "peer_sub_keys": w(ks[17], (DEPTH, 2, PEER_NKEYS, PEER_HALF), PEER_HALF),
        "peer_u": w(ks[18], (DEPTH, PEER_EXPERTS, D_MODEL), D_MODEL),
        "peer_v": w(ks[19], (DEPTH, PEER_EXPERTS, D_MODEL), PEER_HEADS * PEER_TOPK),
    }


def reference(x, positions, norm_mix, w_in, dil_q_norm, dil_k_norm, mla_cq_norm, mla_ckv_norm,
              mla_w_uq, mla_w_ukv, mla_q_norm, mla_k_norm, w_branch_sb, w_branch_dil, w_branch_mla,
              w_out, norm_ffn, peer_w_query, peer_sub_keys, peer_u, peer_v):
    for l in range(DEPTH):
        x = hybrid_layer(x, positions, norm_mix[l], w_in[l], dil_q_norm[l], dil_k_norm[l],
                         mla_cq_norm[l], mla_ckv_norm[l], mla_w_uq[l], mla_w_ukv[l],
                         mla_q_norm[l], mla_k_norm[l], w_branch_sb[l], w_branch_dil[l],
                         w_branch_mla[l], w_out[l], norm_ffn[l], peer_w_query[l],
                         peer_sub_keys[l], peer_u[l], peer_v[l])
    return x
```

```python
import functools

import numpy as np
import jax
import jax.numpy as jnp
from jax import lax
from jax.experimental import pallas as pl
from jax.experimental.pallas import tpu as pltpu

F32 = jnp.float32
BF16 = jnp.bfloat16

D_MODEL = 2048
HEAD_DIM = 128
ROPE_THETA = 10000.0
EPS = 1e-6
NEG_INF = -1e30

SB_HEADS = 4
SB_WIDTH = SB_HEADS * HEAD_DIM
DIL_PATTERNS = ((128, 1), (512, 4), (2048, 16))
DIL_HEADS_PER_GROUP = 2
DIL_HEADS = DIL_HEADS_PER_GROUP * len(DIL_PATTERNS)
DIL_WIDTH = DIL_HEADS * HEAD_DIM
DIL_OUT_WIDTH = DIL_HEADS_PER_GROUP * HEAD_DIM
DIL_BLOCK = 128
MLA_HEADS = 6
MLA_NOPE = 128
MLA_ROPE = 64
MLA_V = 128
MLA_QK = MLA_NOPE + MLA_ROPE
MLA_Q_RANK = 512
MLA_KV_RANK = 256
MLA_HEAD_PAD = 256
N_BRANCH = 3
MIX_COLS = 3 * SB_WIDTH + 3 * DIL_WIDTH + MLA_Q_RANK + MLA_KV_RANK + MLA_ROPE
PEER_HEADS = 8
PEER_NKEYS = 128
PEER_EXPERTS = PEER_NKEYS * PEER_NKEYS
PEER_HALF = 128
PEER_TOPK = 16

SB_EXIT = -104.0

VMEM_LIMIT = 56 * 1024 * 1024


def _cparams(*sem):
    return pltpu.CompilerParams(dimension_semantics=sem, vmem_limit_bytes=VMEM_LIMIT)


def _rms(x, g):
    return x * lax.rsqrt(jnp.mean(x * x, axis=-1, keepdims=True) + EPS) * g


def _rope_table_kernel(pos_ref, inv_ref, sign_ref, cos_ref, sin_ref):
    ang = pos_ref[...].astype(F32) * inv_ref[...]
    cos_ref[...] = jnp.cos(ang)
    sin_ref[...] = jnp.sin(ang) * sign_ref[...]


def rope_tables(pos_col, half, tm=2048):
    T = pos_col.shape[0]
    inv_h = ROPE_THETA ** (-jnp.arange(half, dtype=F32) / half)
    pad = jnp.zeros((128 - 2 * half,), F32)
    inv = jnp.concatenate([inv_h, inv_h, pad])[None, :]
    sign = jnp.concatenate([-jnp.ones((half,), F32), jnp.ones((half,), F32), pad])[None, :]
    row = pl.BlockSpec((1, 128), lambda i: (0, 0))
    cos, sin = pl.pallas_call(
        _rope_table_kernel,
        grid=(T // tm,),
        in_specs=[pl.BlockSpec((tm, 1), lambda i: (i, 0)), row, row],
        out_specs=[pl.BlockSpec((tm, 128), lambda i: (i, 0))] * 2,
        out_shape=[jax.ShapeDtypeStruct((T, 128), F32)] * 2,
        compiler_params=_cparams("parallel"),
        name="rope_tables",
    )(pos_col, inv, sign)
    return cos, sin


def _norm_matmul_kernel(x_ref, g_ref, w_ref, o_ref, u_ref):
    @pl.when(pl.program_id(1) == 0)
    def _():
        u_ref[...] = _rms(x_ref[...], g_ref[...]).astype(BF16)

    o_ref[...] = jnp.dot(u_ref[...], w_ref[...], preferred_element_type=F32).astype(o_ref.dtype)


def norm_matmul(x, g, w, out_dtype, tn, tm=512, name="norm_matmul"):
    T, D = x.shape
    N = w.shape[1]
    return pl.pallas_call(
        _norm_matmul_kernel,
        grid=(T // tm, N // tn),
        in_specs=[pl.BlockSpec((tm, D), lambda i, j: (i, 0)),
                  pl.BlockSpec((1, D), lambda i, j: (0, 0)),
                  pl.BlockSpec((D, tn), lambda i, j: (0, j))],
        out_specs=pl.BlockSpec((tm, tn), lambda i, j: (i, j)),
        out_shape=jax.ShapeDtypeStruct((T, N), out_dtype),
        scratch_shapes=[pltpu.VMEM((tm, D), BF16)],
        compiler_params=_cparams("parallel", "arbitrary"),
        name=name,
    )(x, g[None, :], w)


def _sb_kernel(q_ref, k_ref, v_ref, tri_ref, o_ref, acc_ref, carry_ref, *, tq, tk, scale):
    q0 = pl.program_id(2) * tq
    acc_ref[...] = jnp.zeros_like(acc_ref)
    carry_ref[...] = jnp.zeros_like(carry_ref)
    q = q_ref[...]
    tri = tri_ref[...]
    row = q0 + lax.broadcasted_iota(jnp.int32, (tq, tk), 0)
    lane = lax.broadcasted_iota(jnp.int32, (tq, tk), 1)

    def cond(c):
        kb, top = c
        return jnp.logical_and(kb >= 0, top > SB_EXIT)

    def body(c):
        kb, _ = c
        ks = pl.multiple_of(kb * tk, tk)
        k = k_ref[pl.ds(ks, tk), :]
        v = v_ref[pl.ds(ks, tk), :]
        z = lax.dot_general(q, k, (((1,), (1,)), ((), ())), preferred_element_type=F32) * scale
        mask = (ks + lane) < row
        lk = -(jnp.maximum(z, 0.0) + jnp.log1p(jnp.exp(-jnp.abs(z))))
        lk = jnp.where(mask, lk, 0.0)
        hi = lk.astype(BF16)
        lo = (lk - hi.astype(F32)).astype(BF16)
        loc = (jnp.dot(hi, tri, preferred_element_type=F32)
               + jnp.dot(lo, tri, preferred_element_type=F32))
        carry = carry_ref[...]
        after = carry + loc[:, :tk]
        w = jnp.where(mask, jnp.exp(z + lk + after), 0.0)
        acc_ref[...] += jnp.dot(w.astype(BF16), v, preferred_element_type=F32)
        new_carry = carry + loc[:, tk:]
        carry_ref[...] = new_carry
        return kb - 1, jnp.max(new_carry)

    lax.while_loop(cond, body, ((q0 + tq) // tk - 1, jnp.float32(0.0)))
    o_ref[...] = acc_ref[...].astype(o_ref.dtype)


def sb_attention(qkv, B, S, col0, tq=256, tk=128):
    T = B * S
    nq = S // tq
    tri = np.concatenate([np.triu(np.ones((tk, tk), np.float32), 1).T, np.ones((tk, tk), np.float32)], axis=1)
    tri = jnp.asarray(tri, BF16)
    kern = functools.partial(_sb_kernel, tq=tq, tk=tk, scale=HEAD_DIM ** -0.5)
    return pl.pallas_call(
        kern,
        grid=(B, SB_HEADS, nq),
        in_specs=[pl.BlockSpec((tq, HEAD_DIM), lambda b, h, i: (b * nq + i, col0 + h)),
                  pl.BlockSpec((S, HEAD_DIM), lambda b, h, i: (b, col0 + SB_HEADS + h)),
                  pl.BlockSpec((S, HEAD_DIM), lambda b, h, i: (b, col0 + 2 * SB_HEADS + h)),
                  pl.BlockSpec((tk, 2 * tk), lambda b, h, i: (0, 0))],
        out_specs=pl.BlockSpec((tq, HEAD_DIM), lambda b, h, i: (b * nq + i, h)),
        out_shape=jax.ShapeDtypeStruct((T, SB_WIDTH), BF16),
        scratch_shapes=[pltpu.VMEM((tq, HEAD_DIM), F32), pltpu.VMEM((tq, tk), F32)],
        compiler_params=_cparams("parallel", "parallel", "arbitrary"),
        name="sb_attention",
    )(qkv, qkv, qkv, tri)


def _dil_prep_kernel(x_ref, g_ref, cos_ref, sin_ref, o_ref):
    y = _rms(x_ref[...], g_ref[...])
    o_ref[...] = (y * cos_ref[...] + pltpu.roll(y, HEAD_DIM // 2, 1) * sin_ref[...]).astype(o_ref.dtype)


def dil_prep(dqk, gains, cos, sin, tm=1024):
    T, C = dqk.shape
    nh = C // HEAD_DIM
    return pl.pallas_call(
        _dil_prep_kernel,
        grid=(T // tm, nh),
        in_specs=[pl.BlockSpec((tm, HEAD_DIM), lambda i, j: (i, j)),
                  pl.BlockSpec((None, 1, HEAD_DIM), lambda i, j: (j, 0, 0)),
                  pl.BlockSpec((tm, HEAD_DIM), lambda i, j: (i, 0)),
                  pl.BlockSpec((tm, HEAD_DIM), lambda i, j: (i, 0))],
        out_specs=pl.BlockSpec((tm, HEAD_DIM), lambda i, j: (i, j)),
        out_shape=jax.ShapeDtypeStruct((T, C), BF16),
        compiler_params=_cparams("parallel", "arbitrary"),
        name="dil_prep",
    )(dqk, gains, cos, sin)


def _dil_kernel(q_ref, kp_ref, kc_ref, vp_ref, vc_ref, o_ref, lse_ref, *, n_back, scale):
    n = pl.program_id(2)
    blk = DIL_BLOCK
    qi = lax.broadcasted_iota(jnp.int32, (blk, blk), 0)
    kj = lax.broadcasted_iota(jnp.int32, (blk, blk), 1)
    mask_prev = jnp.logical_and(blk + qi - kj <= n_back, n > 0)
    mask_cur = kj <= qi
    nt = (((1,), (1,)), ((), ()))
    for h in range(DIL_HEADS_PER_GROUP):
        sl = slice(h * HEAD_DIM, (h + 1) * HEAD_DIM)
        q = q_ref[:, sl]
        sp = lax.dot_general(q, kp_ref[:, sl], nt, preferred_element_type=F32) * scale
        sc = lax.dot_general(q, kc_ref[:, sl], nt, preferred_element_type=F32) * scale
        sp = jnp.where(mask_prev, sp, NEG_INF)
        sc = jnp.where(mask_cur, sc, NEG_INF)
        m = jnp.maximum(jnp.max(sp, axis=-1, keepdims=True), jnp.max(sc, axis=-1, keepdims=True))
        ep = jnp.exp(sp - m)
        ec = jnp.exp(sc - m)
        den = jnp.sum(ep, axis=-1, keepdims=True) + jnp.sum(ec, axis=-1, keepdims=True)
        o = (jnp.dot((ep / den).astype(BF16), vp_ref[:, sl], preferred_element_type=F32)
             + jnp.dot((ec / den).astype(BF16), vc_ref[:, sl], preferred_element_type=F32))
        o_ref[:, sl] = o
        lse_ref[:, sl] = jnp.broadcast_to(m + jnp.log(den), (blk, HEAD_DIM))


def dil_attention(dqk, dv, B, S, g, v_col0):
    window, d = DIL_PATTERNS[g]
    L = S // d
    nb = L // DIL_BLOCK
    cq = dqk.shape[1] // DIL_OUT_WIDTH
    cv = dv.shape[1] // DIL_OUT_WIDTH
    q3 = dqk.reshape(B, L, d * dqk.shape[1])
    v3 = dv.reshape(B, L, d * dv.shape[1])
    blk = (None, DIL_BLOCK, DIL_OUT_WIDTH)
    kg = len(DIL_PATTERNS) + g
    kern = functools.partial(_dil_kernel, n_back=window // d, scale=HEAD_DIM ** -0.5)
    o, lse = pl.pallas_call(
        kern,
        grid=(B, d, nb),
        in_specs=[pl.BlockSpec(blk, lambda b, r, n: (b, n, r * cq + g)),
                  pl.BlockSpec(blk, lambda b, r, n: (b, jnp.maximum(n - 1, 0), r * cq + kg)),
                  pl.BlockSpec(blk, lambda b, r, n: (b, n, r * cq + kg)),
                  pl.BlockSpec(blk, lambda b, r, n: (b, jnp.maximum(n - 1, 0), r * cv + v_col0 + g)),
                  pl.BlockSpec(blk, lambda b, r, n: (b, n, r * cv + v_col0 + g))],
        out_specs=[pl.BlockSpec(blk, lambda b, r, n: (b, n, r))] * 2,
        out_shape=[jax.ShapeDtypeStruct((B, L, d * DIL_OUT_WIDTH), F32)] * 2,
        compiler_params=_cparams("parallel", "parallel", "arbitrary"),
        name=f"dil_attention_{g}",
    )(q3, q3, q3, v3, v3)
    return o.reshape(B * S, DIL_OUT_WIDTH), lse.reshape(B * S, DIL_OUT_WIDTH)


def _dil_merge_kernel(o0, o1, o2, l0, l1, l2, out_ref):
    a0, a1, a2 = l0[...], l1[...], l2[...]
    m = jnp.maximum(jnp.maximum(a0, a1), a2)
    e0, e1, e2 = jnp.exp(a0 - m), jnp.exp(a1 - m), jnp.exp(a2 - m)
    den = e0 + e1 + e2
    out_ref[...] = ((e0 / den) * o0[...] + (e1 / den) * o1[...] + (e2 / den) * o2[...]).astype(out_ref.dtype)


def dil_merge(outs, lses, tm=1024):
    T = outs[0].shape[0]
    spec = pl.BlockSpec((tm, DIL_OUT_WIDTH), lambda i: (i, 0))
    return pl.pallas_call(
        _dil_merge_kernel,
        grid=(T // tm,),
        in_specs=[spec] * 6,
        out_specs=spec,
        out_shape=jax.ShapeDtypeStruct((T, DIL_OUT_WIDTH), BF16),
        compiler_params=_cparams("parallel"),
        name="dil_merge",
    )(*outs, *lses)


def _mla_prep_kernel(lat_ref, gcq_ref, gckv_ref, wqn_ref, wqr_ref, wkv_ref, gq_ref, gk_ref,
                     cos_ref, sin_ref, q_ref, k_ref, v_ref):
    lat = lat_ref[...]
    cq = _rms(lat[:, :MLA_Q_RANK], gcq_ref[...]).astype(BF16)
    ckv = _rms(lat[:, MLA_Q_RANK:MLA_Q_RANK + MLA_KV_RANK], gckv_ref[...]).astype(BF16)
    kpe = lat[:, MLA_Q_RANK + MLA_KV_RANK:]
    qn = jnp.dot(cq, wqn_ref[...], preferred_element_type=F32)
    qr = jnp.dot(cq, wqr_ref[...], preferred_element_type=F32)
    kv = jnp.dot(ckv, wkv_ref[...], preferred_element_type=F32)
    cos, sin = cos_ref[...], sin_ref[...]
    gq, gk = gq_ref[...], gk_ref[...]
    kpe_ss = jnp.sum(kpe * kpe, axis=-1, keepdims=True)

    def rope64(y):
        return y * cos + (pltpu.roll(y, 32, 1) + pltpu.roll(y, 96, 1)) * sin

    for h in range(MLA_HEADS):
        a = slice(h * 128, (h + 1) * 128)
        n0 = h * MLA_HEAD_PAD
        qn_h, qr_h = qn[:, a], qr[:, a]
        ms = (jnp.sum(qn_h * qn_h, axis=-1, keepdims=True)
              + jnp.sum(qr_h * qr_h, axis=-1, keepdims=True)) * (1.0 / MLA_QK)
        r = lax.rsqrt(ms + EPS)
        q_ref[:, n0:n0 + 128] = (qn_h * r * gq[:, :128]).astype(BF16)
        q_ref[:, n0 + 128:n0 + 256] = rope64(qr_h * r * gq[:, 128:]).astype(BF16)
        kn_h = kv[:, n0:n0 + 128]
        ms = (jnp.sum(kn_h * kn_h, axis=-1, keepdims=True) + kpe_ss) * (1.0 / MLA_QK)
        r = lax.rsqrt(ms + EPS)
        k_ref[:, n0:n0 + 128] = (kn_h * r * gk[:, :128]).astype(BF16)
        k_ref[:, n0 + 128:n0 + 256] = rope64(kpe * r * gk[:, 128:]).astype(BF16)
        v_ref[:, a] = kv[:, n0 + 128:n0 + 256].astype(BF16)


def mla_prep(lat, gcq, gckv, wqn, wqr, wkv, gq, gk, cos, sin, tm=512):
    T = lat.shape[0]
    full = lambda a: pl.BlockSpec(a.shape, lambda i: (0,) * a.ndim)
    rows = lambda w: pl.BlockSpec((tm, w), lambda i: (i, 0))
    return pl.pallas_call(
        _mla_prep_kernel,
        grid=(T // tm,),
        in_specs=[rows(lat.shape[1]), full(gcq), full(gckv), full(wqn), full(wqr), full(wkv),
                  full(gq), full(gk), rows(128), rows(128)],
        out_specs=[rows(MLA_HEADS * MLA_HEAD_PAD), rows(MLA_HEADS * MLA_HEAD_PAD), rows(MLA_HEADS * MLA_V)],
        out_shape=[jax.ShapeDtypeStruct((T, MLA_HEADS * MLA_HEAD_PAD), BF16),
                   jax.ShapeDtypeStruct((T, MLA_HEADS * MLA_HEAD_PAD), BF16),
                   jax.ShapeDtypeStruct((T, MLA_HEADS * MLA_V), BF16)],
        compiler_params=_cparams("parallel"),
        name="mla_prep",
    )(lat, gcq, gckv, wqn, wqr, wkv, gq, gk, cos, sin)


def _flash_kernel(q_ref, k_ref, v_ref, o_ref, acc_ref, m_ref, l_ref, *, t, scale):
    qi = pl.program_id(2)
    q = q_ref[...]
    acc_ref[...] = jnp.zeros_like(acc_ref)
    m_ref[...] = jnp.full_like(m_ref, NEG_INF)
    l_ref[...] = jnp.zeros_like(l_ref)
    nt = (((1,), (1,)), ((), ()))

    def step(ks, masked):
        k = k_ref[pl.ds(ks, t), :]
        v = v_ref[pl.ds(ks, t), :]
        s = lax.dot_general(q, k, nt, preferred_element_type=F32) * scale
        if masked:
            r = lax.broadcasted_iota(jnp.int32, (t, t), 0)
            c = lax.broadcasted_iota(jnp.int32, (t, t), 1)
            s = jnp.where(c <= r, s, NEG_INF)
        m_old = m_ref[...]
        m_new = jnp.maximum(m_old, jnp.max(s, axis=-1, keepdims=True))
        alpha = jnp.exp(m_old - m_new)
        p = jnp.exp(s - m_new)
        l_ref[...] = alpha * l_ref[...] + jnp.sum(p, axis=-1, keepdims=True)
        acc_ref[...] = alpha * acc_ref[...] + jnp.dot(p.astype(BF16), v, preferred_element_type=F32)
        m_ref[...] = m_new

    def body(kb, carry):
        step(pl.multiple_of(kb * t, t), False)
        return carry

    lax.fori_loop(0, qi, body, 0)
    step(pl.multiple_of(qi * t, t), True)
    o_ref[...] = (acc_ref[...] / l_ref[...]).astype(o_ref.dtype)


def mla_flash(q, k, v, B, S, t=512):
    T = B * S
    nq = S // t
    kern = functools.partial(_flash_kernel, t=t, scale=MLA_QK ** -0.5)
    return pl.pallas_call(
        kern,
        grid=(B, MLA_HEADS, nq),
        in_specs=[pl.BlockSpec((t, MLA_HEAD_PAD), lambda b, h, i: (b * nq + i, h)),
                  pl.BlockSpec((S, MLA_HEAD_PAD), lambda b, h, i: (b, h)),
                  pl.BlockSpec((S, MLA_V), lambda b, h, i: (b, h))],
        out_specs=pl.BlockSpec((t, MLA_V), lambda b, h, i: (b * nq + i, h)),
        out_shape=jax.ShapeDtypeStruct((T, MLA_HEADS * MLA_V), BF16),
        scratch_shapes=[pltpu.VMEM((t, MLA_V), F32), pltpu.VMEM((t, 1), F32), pltpu.VMEM((t, 1), F32)],
        compiler_params=_cparams("parallel", "parallel", "arbitrary"),
        name="mla_flash",
    )(q, k, v)


def _merge_kernel(x_ref, g_ref, osb_ref, odil_ref, omla_ref, wg0_ref, wg1_ref, wg2_ref,
                  wsb_ref, wdil_ref, wmla_ref, wout_ref, o_ref, u_ref):
    j = pl.program_id(1)

    @pl.when(j == 0)
    def _():
        x = x_ref[...]
        u_ref[...] = _rms(x, g_ref[...]).astype(BF16)
        o_ref[...] = x

    u = u_ref[...]

    def branch(o_b, wg, wb):
        gate = jax.nn.sigmoid(jnp.dot(u, wg[...], preferred_element_type=F32))
        return gate * jnp.dot(o_b[...], wb[...], preferred_element_type=F32)

    merged = (branch(osb_ref, wg0_ref, wsb_ref) + branch(odil_ref, wg1_ref, wdil_ref)
              + branch(omla_ref, wg2_ref, wmla_ref))
    o_ref[...] += jnp.dot(merged.astype(BF16), wout_ref[...], preferred_element_type=F32)


def merge_out(x, g, osb, odil, omla, wg, wsb, wdil, wmla, wout, tm=512, tn=256):
    T, D = x.shape
    nj = D // tn
    rows = lambda w: pl.BlockSpec((tm, w), lambda i, j: (i, 0))
    cols = lambda k: pl.BlockSpec((k, tn), lambda i, j: (0, j))
    gcol = lambda b: pl.BlockSpec((D, tn), lambda i, j: (0, b * nj + j))
    return pl.pallas_call(
        _merge_kernel,
        grid=(T // tm, nj),
        in_specs=[rows(D), pl.BlockSpec((1, D), lambda i, j: (0, 0)),
                  rows(osb.shape[1]), rows(odil.shape[1]), rows(omla.shape[1]),
                  gcol(0), gcol(1), gcol(2),
                  cols(wsb.shape[0]), cols(wdil.shape[0]), cols(wmla.shape[0]),
                  pl.BlockSpec((tn, D), lambda i, j: (j, 0))],
        out_specs=rows(D),
        out_shape=jax.ShapeDtypeStruct((T, D), F32),
        scratch_shapes=[pltpu.VMEM((tm, D), BF16)],
        compiler_params=_cparams("parallel", "arbitrary"),
        name="merge_out",
    )(x, g[None, :], osb, odil, omla, wg, wg, wg, wsb, wdil, wmla, wout)


def _top_desc(s, k):
    vals = []
    for _ in range(k):
        mx = jnp.max(s, axis=0, keepdims=True)
        vals.append(mx)
        s = jnp.where(s == mx, -jnp.inf, s)
    return jnp.concatenate(vals, axis=0)


def _peer_front_kernel(h_ref, g_ref, wq_ref, keys_ref, xT_ref, s1_ref, s2_ref, tau_ref, lse_ref):
    hn = _rms(h_ref[...], g_ref[...])
    xT_ref[...] = hn.T.astype(BF16)
    q = jnp.dot(hn.astype(BF16), wq_ref[...], preferred_element_type=F32).astype(BF16)
    nt = (((1,), (1,)), ((), ()))
    taus, lses = [], []
    for h in range(PEER_HEADS):
        c0 = h * 2 * PEER_HALF
        s1 = lax.dot_general(keys_ref[0], q[:, c0:c0 + PEER_HALF], nt, preferred_element_type=F32)
        s2 = lax.dot_general(keys_ref[1], q[:, c0 + PEER_HALF:c0 + 2 * PEER_HALF], nt,
                             preferred_element_type=F32)
        s1_ref[h] = s1
        s2_ref[h] = s2
        t1 = _top_desc(s1, PEER_TOPK)
        t2 = _top_desc(s2, PEER_TOPK)
        cand = jnp.concatenate([t1[a:a + 1] + t2 for a in range(PEER_TOPK)], axis=0)
        best = _top_desc(cand, PEER_TOPK)
        tau = best[PEER_TOPK - 1:PEER_TOPK]
        mx = best[0:1]
        z = jnp.sum(jnp.where(cand >= tau, jnp.exp(cand - mx), 0.0), axis=0, keepdims=True)
        taus.append(tau)
        lses.append(mx + jnp.log(z))
    tau_ref[...] = jnp.concatenate(taus, axis=0)
    lse_ref[...] = jnp.concatenate(lses, axis=0)


def peer_front(h, g, wq, keys, tm=256):
    T, D = h.shape
    tok = lambda: pl.BlockSpec((PEER_HEADS, PEER_NKEYS, tm), lambda i: (0, 0, i))
    row = lambda: pl.BlockSpec((PEER_HEADS, tm), lambda i: (0, i))
    return pl.pallas_call(
        _peer_front_kernel,
        grid=(T // tm,),
        in_specs=[pl.BlockSpec((tm, D), lambda i: (i, 0)),
                  pl.BlockSpec((1, D), lambda i: (0, 0)),
                  pl.BlockSpec(wq.shape, lambda i: (0, 0)),
                  pl.BlockSpec(keys.shape, lambda i: (0, 0, 0))],
        out_specs=[pl.BlockSpec((D, tm), lambda i: (0, i)), tok(), tok(), row(), row()],
        out_shape=[jax.ShapeDtypeStruct((D, T), BF16),
                   jax.ShapeDtypeStruct((PEER_HEADS, PEER_NKEYS, T), F32),
                   jax.ShapeDtypeStruct((PEER_HEADS, PEER_NKEYS, T), F32),
                   jax.ShapeDtypeStruct((PEER_HEADS, T), F32),
                   jax.ShapeDtypeStruct((PEER_HEADS, T), F32)],
        compiler_params=_cparams("parallel"),
        name="peer_front",
    )(h, g[None, :], wq, keys)


def _transpose_cast_kernel(x_ref, o_ref):
    o_ref[...] = x_ref[...].T.astype(o_ref.dtype)


def transpose_cast(x, dtype, tr=512):
    R, C = x.shape
    return pl.pallas_call(
        _transpose_cast_kernel,
        grid=(R // tr,),
        in_specs=[pl.BlockSpec((tr, C), lambda i: (i, 0))],
        out_specs=pl.BlockSpec((C, tr), lambda i: (0, i)),
        out_shape=jax.ShapeDtypeStruct((C, R), dtype),
        compiler_params=_cparams("parallel"),
        name="transpose_cast",
    )(x)


def _peer_dense_kernel(xT_ref, u_ref, vT_ref, s1_ref, s2_ref, tau_ref, lse_ref, h_ref, o_ref, acc_ref,
                       *, te):
    e = pl.program_id(1)

    @pl.when(e == 0)
    def _():
        acc_ref[...] = jnp.zeros_like(acc_ref)

    aT = jnp.dot(u_ref[...], xT_ref[...], preferred_element_type=F32)
    tau = tau_ref[...]
    lse = lse_ref[...]
    parts = []
    for c in range(te // PEER_NKEYS):
        i = e * (te // PEER_NKEYS) + c
        gate = None
        for h in range(PEER_HEADS):
            cand = s2_ref[h] + s1_ref[h, pl.ds(i, 1), :]
            w = jnp.where(cand >= tau[h:h + 1], jnp.exp(cand - lse[h:h + 1]), 0.0)
            gate = w if gate is None else gate + w
        a = aT[c * PEER_NKEYS:(c + 1) * PEER_NKEYS]
        gelu = 0.5 * a * (1.0 + lax.erf(a * (2.0 ** -0.5)))
        parts.append((gelu * gate).astype(BF16))
    hT = jnp.concatenate(parts, axis=0)
    acc_ref[...] += jnp.dot(vT_ref[...], hT, preferred_element_type=F32)

    @pl.when(e == pl.num_programs(1) - 1)
    def _():
        o_ref[...] = h_ref[...] + acc_ref[...].T


def peer_dense(xT, u, vT, s1, s2, tau, lse, h, tm=512, te=512):
    D, T = xT.shape
    E = u.shape[0]
    kern = functools.partial(_peer_dense_kernel, te=te)
    tok = lambda: pl.BlockSpec((PEER_HEADS, PEER_NKEYS, tm), lambda i, e: (0, 0, i))
    row = lambda: pl.BlockSpec((PEER_HEADS, tm), lambda i, e: (0, i))
    return pl.pallas_call(
        kern,
        grid=(T // tm, E // te),
        in_specs=[pl.BlockSpec((D, tm), lambda i, e: (0, i)),
                  pl.BlockSpec((te, D), lambda i, e: (e, 0)),
                  pl.BlockSpec((D, te), lambda i, e: (0, e)),
                  tok(), tok(), row(), row(),
                  pl.BlockSpec((tm, D), lambda i, e: (i, 0))],
        out_specs=pl.BlockSpec((tm, D), lambda i, e: (i, 0)),
        out_shape=jax.ShapeDtypeStruct((T, D), F32),
        scratch_shapes=[pltpu.VMEM((D, tm), F32)],
        compiler_params=_cparams("parallel", "arbitrary"),
        name="peer_dense",
    )(xT, u, vT, s1, s2, tau, lse, h)


def _pad_cols(w, n):
    return jnp.pad(w, ((0, 0), (0, n - w.shape[1])))


def _layer(x, B, S, tabs, norm_mix, w_in, dil_q_norm, dil_k_norm, mla_cq_norm, mla_ckv_norm, mla_w_uq,
           mla_w_ukv, mla_q_norm, mla_k_norm, w_branch_sb, w_branch_dil, w_branch_mla, w_out, norm_ffn,
           peer_w_query, peer_sub_keys, peer_u, peer_v):
    cos128, sin128, cos64, sin64 = tabs
    o_sbq, o_dq, o_dv = 0, 3 * SB_WIDTH, 3 * SB_WIDTH + 2 * DIL_WIDTH
    o_cq = 3 * SB_WIDTH + 3 * DIL_WIDTH

    w_a = jnp.concatenate([w_in[:, o_sbq:o_dq], w_in[:, o_dv:o_cq]], axis=1).astype(BF16)
    w_b = w_in[:, o_dq:o_dv].astype(BF16)
    w_c = _pad_cols(w_in[:, o_cq:MIX_COLS], 896).astype(BF16)
    w_g = w_in[:, MIX_COLS:].astype(BF16)
    p_a = norm_matmul(x, norm_mix, w_a, BF16, tn=768, name="in_proj_a")
    p_b = norm_matmul(x, norm_mix, w_b, F32, tn=768, name="in_proj_b")
    p_c = norm_matmul(x, norm_mix, w_c, F32, tn=896, name="in_proj_c")

    o_sb = sb_attention(p_a, B, S, col0=0)

    gains = jnp.concatenate([jnp.broadcast_to(dil_q_norm, (DIL_HEADS, HEAD_DIM)),
                             jnp.broadcast_to(dil_k_norm, (DIL_HEADS, HEAD_DIM))])[:, None, :]
    dqk = dil_prep(p_b, gains, cos128, sin128)
    outs, lses = zip(*[dil_attention(dqk, p_a, B, S, g, v_col0=3 * SB_WIDTH // DIL_OUT_WIDTH)
                       for g in range(len(DIL_PATTERNS))])
    o_dil = dil_merge(outs, lses)

    uq = mla_w_uq.reshape(MLA_Q_RANK, MLA_HEADS, MLA_QK)
    wqn = uq[:, :, :MLA_NOPE].reshape(MLA_Q_RANK, MLA_HEADS * MLA_NOPE).astype(BF16)
    wqr = jnp.pad(uq[:, :, MLA_NOPE:], ((0, 0), (0, 0), (0, 128 - MLA_ROPE)))
    wqr = wqr.reshape(MLA_Q_RANK, MLA_HEADS * 128).astype(BF16)
    gpad = lambda gn: jnp.pad(gn, (0, MLA_HEAD_PAD - MLA_QK))[None, :]
    q, k, v = mla_prep(p_c, mla_cq_norm[None, :], mla_ckv_norm[None, :], wqn, wqr, mla_w_ukv.astype(BF16),
                       gpad(mla_q_norm), gpad(mla_k_norm), cos64, sin64)
    o_mla = mla_flash(q, k, v, B, S)

    h = merge_out(x, norm_mix, o_sb, o_dil, o_mla, w_g, w_branch_sb.astype(BF16), w_branch_dil.astype(BF16),
                  w_branch_mla.astype(BF16), w_out.astype(BF16))

    xT, s1, s2, tau, lse = peer_front(h, norm_ffn, peer_w_query.astype(BF16), peer_sub_keys.astype(BF16))
    return peer_dense(xT, peer_u.astype(BF16), transpose_cast(peer_v, BF16), s1, s2, tau, lse, h)


def kernel(x, positions, norm_mix, w_in, dil_q_norm, dil_k_norm, mla_cq_norm, mla_ckv_norm, mla_w_uq, mla_w_ukv,
           mla_q_norm, mla_k_norm, w_branch_sb, w_branch_dil, w_branch_mla, w_out, norm_ffn, peer_w_query,
           peer_sub_keys, peer_u, peer_v):
    B, S, D = x.shape
    pos_col = positions.reshape(B * S, 1)
    tabs = rope_tables(pos_col, HEAD_DIM // 2) + rope_tables(pos_col, MLA_ROPE // 2)
    xf = x.reshape(B * S, D)
    for l in range(w_in.shape[0]):
        xf = _layer(xf, B, S, tabs, norm_mix[l], w_in[l], dil_q_norm[l], dil_k_norm[l], mla_cq_norm[l],
                    mla_ckv_norm[l], mla_w_uq[l], mla_w_ukv[l], mla_q_norm[l], mla_k_norm[l], w_branch_sb[l],
                    w_branch_dil[l], w_branch_mla[l], w_out[l], norm_ffn[l], peer_w_query[l],
                    peer_sub_keys[l], peer_u[l], peer_v[l])
    return xf.reshape(B, S, D)
```

```python
import functools

import numpy as np
import jax
import jax.numpy as jnp
from jax import lax
from jax.experimental import pallas as pl
from jax.experimental.pallas import tpu as pltpu

F32 = jnp.float32
BF16 = jnp.bfloat16

D_MODEL = 2048
HEAD_DIM = 128
ROPE_THETA = 10000.0
EPS = 1e-6
NEG_INF = -1e30
LOG2E = 1.4426950408889634

SB_HEADS = 4
SB_WIDTH = SB_HEADS * HEAD_DIM
DIL_PATTERNS = ((128, 1), (512, 4), (2048, 16))
DIL_HEADS_PER_GROUP = 2
DIL_HEADS = DIL_HEADS_PER_GROUP * len(DIL_PATTERNS)
DIL_WIDTH = DIL_HEADS * HEAD_DIM
DIL_OUT_WIDTH = DIL_HEADS_PER_GROUP * HEAD_DIM
DIL_BLOCK = 128
MLA_HEADS = 6
MLA_NOPE = 128
MLA_ROPE = 64
MLA_V = 128
MLA_QK = MLA_NOPE + MLA_ROPE
MLA_Q_RANK = 512
MLA_KV_RANK = 256
MLA_HEAD_PAD = 256
N_BRANCH = 3
MIX_COLS = 3 * SB_WIDTH + 3 * DIL_WIDTH + MLA_Q_RANK + MLA_KV_RANK + MLA_ROPE
PEER_HEADS = 8
PEER_NKEYS = 128
PEER_EXPERTS = PEER_NKEYS * PEER_NKEYS
PEER_HALF = 128
PEER_TOPK = 16

SB_EXIT = -104.0

FLASH_SLACK = 60.0

VMEM_LIMIT = 56 * 1024 * 1024


def _cparams(*sem):
    return pltpu.CompilerParams(dimension_semantics=sem, vmem_limit_bytes=VMEM_LIMIT)


def _rms(x, g):
    return x * lax.rsqrt(jnp.mean(x * x, axis=-1, keepdims=True) + EPS) * g


def _rope_table_kernel(pos_ref, inv_ref, sign_ref, cos_ref, sin_ref):
    ang = pos_ref[...].astype(F32) * inv_ref[...]
    cos_ref[...] = jnp.cos(ang)
    sin_ref[...] = jnp.sin(ang) * sign_ref[...]


def rope_tables(pos_col, half, tm=2048):
    T = pos_col.shape[0]
    inv_h = ROPE_THETA ** (-jnp.arange(half, dtype=F32) / half)
    pad = jnp.zeros((128 - 2 * half,), F32)
    inv = jnp.concatenate([inv_h, inv_h, pad])[None, :]
    sign = jnp.concatenate([-jnp.ones((half,), F32), jnp.ones((half,), F32), pad])[None, :]
    row = pl.BlockSpec((1, 128), lambda i: (0, 0))
    cos, sin = pl.pallas_call(
        _rope_table_kernel,
        grid=(T // tm,),
        in_specs=[pl.BlockSpec((tm, 1), lambda i: (i, 0)), row, row],
        out_specs=[pl.BlockSpec((tm, 128), lambda i: (i, 0))] * 2,
        out_shape=[jax.ShapeDtypeStruct((T, 128), F32)] * 2,
        compiler_params=_cparams("parallel"),
        name="rope_tables",
    )(pos_col, inv, sign)
    return cos, sin


def _norm_matmul_kernel(x_ref, g_ref, w_ref, o_ref, u_ref):
    @pl.when(pl.program_id(1) == 0)
    def _():
        u_ref[...] = _rms(x_ref[...], g_ref[...]).astype(BF16)

    o_ref[...] = jnp.dot(u_ref[...], w_ref[...], preferred_element_type=F32).astype(o_ref.dtype)


def norm_matmul(x, g, w, out_dtype, tn, tm=512, name="norm_matmul"):
    T, D = x.shape
    N = w.shape[1]
    return pl.pallas_call(
        _norm_matmul_kernel,
        grid=(T // tm, N // tn),
        in_specs=[pl.BlockSpec((tm, D), lambda i, j: (i, 0)),
                  pl.BlockSpec((1, D), lambda i, j: (0, 0)),
                  pl.BlockSpec((D, tn), lambda i, j: (0, j))],
        out_specs=pl.BlockSpec((tm, tn), lambda i, j: (i, j)),
        out_shape=jax.ShapeDtypeStruct((T, N), out_dtype),
        scratch_shapes=[pltpu.VMEM((tm, D), BF16)],
        compiler_params=_cparams("parallel", "arbitrary"),
        name=name,
    )(x, g[None, :], w)


def _sb_kernel(q_ref, k_ref, v_ref, tri_ref, o_ref, acc_ref, carry_ref, *, tq, tk, scale):
    q0 = pl.program_id(2) * tq
    acc_ref[...] = jnp.zeros_like(acc_ref)
    carry_ref[...] = jnp.zeros_like(carry_ref)
    q = q_ref[...]
    tri = tri_ref[...]
    row = q0 + lax.broadcasted_iota(jnp.int32, (tq, tk), 0)
    lane = lax.broadcasted_iota(jnp.int32, (tq, tk), 1)

    def cond(c):
        kb, top = c
        return jnp.logical_and(kb >= 0, top > SB_EXIT)

    def body(c):
        kb, _ = c
        ks = pl.multiple_of(kb * tk, tk)
        k = k_ref[pl.ds(ks, tk), :]
        v = v_ref[pl.ds(ks, tk), :]
        z = lax.dot_general(q, k, (((1,), (1,)), ((), ())), preferred_element_type=F32) * scale
        mask = (ks + lane) < row
        lk = -(jnp.maximum(z, 0.0) + jnp.log1p(jnp.exp(-jnp.abs(z))))
        lk = jnp.where(mask, lk, 0.0)
        hi = lk.astype(BF16)
        lo = (lk - hi.astype(F32)).astype(BF16)
        loc = (jnp.dot(hi, tri, preferred_element_type=F32)
               + jnp.dot(lo, tri, preferred_element_type=F32))
        carry = carry_ref[...]
        after = carry + loc[:, :tk]
        w = jnp.where(mask, jnp.exp(z + lk + after), 0.0)
        acc_ref[...] += jnp.dot(w.astype(BF16), v, preferred_element_type=F32)
        new_carry = carry + loc[:, tk:]
        carry_ref[...] = new_carry
        return kb - 1, jnp.max(new_carry)

    lax.while_loop(cond, body, ((q0 + tq) // tk - 1, jnp.float32(0.0)))
    o_ref[...] = acc_ref[...].astype(o_ref.dtype)


def sb_attention(qkv, B, S, col0, tq=256, tk=128):
    T = B * S
    nq = S // tq
    tri = np.concatenate([np.triu(np.ones((tk, tk), np.float32), 1).T, np.ones((tk, tk), np.float32)], axis=1)
    tri = jnp.asarray(tri, BF16)
    kern = functools.partial(_sb_kernel, tq=tq, tk=tk, scale=HEAD_DIM ** -0.5)
    return pl.pallas_call(
        kern,
        grid=(B, SB_HEADS, nq),
        in_specs=[pl.BlockSpec((tq, HEAD_DIM), lambda b, h, i: (b * nq + i, col0 + h)),
                  pl.BlockSpec((S, HEAD_DIM), lambda b, h, i: (b, col0 + SB_HEADS + h)),
                  pl.BlockSpec((S, HEAD_DIM), lambda b, h, i: (b, col0 + 2 * SB_HEADS + h)),
                  pl.BlockSpec((tk, 2 * tk), lambda b, h, i: (0, 0))],
        out_specs=pl.BlockSpec((tq, HEAD_DIM), lambda b, h, i: (b * nq + i, h)),
        out_shape=jax.ShapeDtypeStruct((T, SB_WIDTH), BF16),
        scratch_shapes=[pltpu.VMEM((tq, HEAD_DIM), F32), pltpu.VMEM((tq, tk), F32)],
        compiler_params=_cparams("parallel", "parallel", "arbitrary"),
        name="sb_attention",
    )(qkv, qkv, qkv, tri)


def _dil_prep_kernel(x_ref, g_ref, cos_ref, sin_ref, o_ref):
    y = _rms(x_ref[...], g_ref[...])
    o_ref[...] = (y * cos_ref[...] + pltpu.roll(y, HEAD_DIM // 2, 1) * sin_ref[...]).astype(o_ref.dtype)


def dil_prep(dqk, gains, cos, sin, tm=1024):
    T, C = dqk.shape
    nh = C // HEAD_DIM
    return pl.pallas_call(
        _dil_prep_kernel,
        grid=(T // tm, nh),
        in_specs=[pl.BlockSpec((tm, HEAD_DIM), lambda i, j: (i, j)),
                  pl.BlockSpec((None, 1, HEAD_DIM), lambda i, j: (j, 0, 0)),
                  pl.BlockSpec((tm, HEAD_DIM), lambda i, j: (i, 0)),
                  pl.BlockSpec((tm, HEAD_DIM), lambda i, j: (i, 0))],
        out_specs=pl.BlockSpec((tm, HEAD_DIM), lambda i, j: (i, j)),
        out_shape=jax.ShapeDtypeStruct((T, C), BF16),
        compiler_params=_cparams("parallel", "arbitrary"),
        name="dil_prep",
    )(dqk, gains, cos, sin)


def _dil_kernel(q_ref, kp_ref, kc_ref, vp_ref, vc_ref, o_ref, lse_ref, *, n_back, scale):
    n = pl.program_id(2)
    blk = DIL_BLOCK
    qi = lax.broadcasted_iota(jnp.int32, (blk, blk), 0)
    kj = lax.broadcasted_iota(jnp.int32, (blk, blk), 1)
    mask_prev = jnp.logical_and(blk + qi - kj <= n_back, n > 0)
    mask_cur = kj <= qi
    nt = (((1,), (1,)), ((), ()))
    for h in range(DIL_HEADS_PER_GROUP):
        sl = slice(h * HEAD_DIM, (h + 1) * HEAD_DIM)
        q = q_ref[:, sl]
        sp = lax.dot_general(q, kp_ref[:, sl], nt, preferred_element_type=F32) * scale
        sc = lax.dot_general(q, kc_ref[:, sl], nt, preferred_element_type=F32) * scale
        sp = jnp.where(mask_prev, sp, NEG_INF)
        sc = jnp.where(mask_cur, sc, NEG_INF)
        m = jnp.maximum(jnp.max(sp, axis=-1, keepdims=True), jnp.max(sc, axis=-1, keepdims=True))
        ep = jnp.exp(sp - m)
        ec = jnp.exp(sc - m)
        den = jnp.sum(ep, axis=-1, keepdims=True) + jnp.sum(ec, axis=-1, keepdims=True)
        o = (jnp.dot((ep / den).astype(BF16), vp_ref[:, sl], preferred_element_type=F32)
             + jnp.dot((ec / den).astype(BF16), vc_ref[:, sl], preferred_element_type=F32))
        o_ref[:, sl] = o
        lse_ref[:, sl] = jnp.broadcast_to(m + jnp.log(den), (blk, HEAD_DIM))


def dil_attention(dqk, dv, B, S, g, v_col0):
    window, d = DIL_PATTERNS[g]
    L = S // d
    nb = L // DIL_BLOCK
    cq = dqk.shape[1] // DIL_OUT_WIDTH
    cv = dv.shape[1] // DIL_OUT_WIDTH
    q3 = dqk.reshape(B, L, d * dqk.shape[1])
    v3 = dv.reshape(B, L, d * dv.shape[1])
    blk = (None, DIL_BLOCK, DIL_OUT_WIDTH)
    kg = len(DIL_PATTERNS) + g
    kern = functools.partial(_dil_kernel, n_back=window // d, scale=HEAD_DIM ** -0.5)
    o, lse = pl.pallas_call(
        kern,
        grid=(B, d, nb),
        in_specs=[pl.BlockSpec(blk, lambda b, r, n: (b, n, r * cq + g)),
                  pl.BlockSpec(blk, lambda b, r, n: (b, jnp.maximum(n - 1, 0), r * cq + kg)),
                  pl.BlockSpec(blk, lambda b, r, n: (b, n, r * cq + kg)),
                  pl.BlockSpec(blk, lambda b, r, n: (b, jnp.maximum(n - 1, 0), r * cv + v_col0 + g)),
                  pl.BlockSpec(blk, lambda b, r, n: (b, n, r * cv + v_col0 + g))],
        out_specs=[pl.BlockSpec(blk, lambda b, r, n: (b, n, r))] * 2,
        out_shape=[jax.ShapeDtypeStruct((B, L, d * DIL_OUT_WIDTH), F32)] * 2,
        compiler_params=_cparams("parallel", "parallel", "arbitrary"),
        name=f"dil_attention_{g}",
    )(q3, q3, q3, v3, v3)
    return o.reshape(B * S, DIL_OUT_WIDTH), lse.reshape(B * S, DIL_OUT_WIDTH)


def _dil_merge_kernel(o0, o1, o2, l0, l1, l2, out_ref):
    a0, a1, a2 = l0[...], l1[...], l2[...]
    m = jnp.maximum(jnp.maximum(a0, a1), a2)
    e0, e1, e2 = jnp.exp(a0 - m), jnp.exp(a1 - m), jnp.exp(a2 - m)
    den = e0 + e1 + e2
    out_ref[...] = ((e0 / den) * o0[...] + (e1 / den) * o1[...] + (e2 / den) * o2[...]).astype(out_ref.dtype)


def dil_merge(outs, lses, tm=1024):
    T = outs[0].shape[0]
    spec = pl.BlockSpec((tm, DIL_OUT_WIDTH), lambda i: (i, 0))
    return pl.pallas_call(
        _dil_merge_kernel,
        grid=(T // tm,),
        in_specs=[spec] * 6,
        out_specs=spec,
        out_shape=jax.ShapeDtypeStruct((T, DIL_OUT_WIDTH), BF16),
        compiler_params=_cparams("parallel"),
        name="dil_merge",
    )(*outs, *lses)


def _mla_prep_kernel(lat_ref, gcq_ref, gckv_ref, wqn_ref, wqr_ref, wkv_ref, gq_ref, gk_ref,
                     cos_ref, sin_ref, q_ref, k_ref, v_ref):
    lat = lat_ref[...]
    cq = _rms(lat[:, :MLA_Q_RANK], gcq_ref[...]).astype(BF16)
    ckv = _rms(lat[:, MLA_Q_RANK:MLA_Q_RANK + MLA_KV_RANK], gckv_ref[...]).astype(BF16)
    kpe = lat[:, MLA_Q_RANK + MLA_KV_RANK:]
    qn = jnp.dot(cq, wqn_ref[...], preferred_element_type=F32)
    qr = jnp.dot(cq, wqr_ref[...], preferred_element_type=F32)
    kv = jnp.dot(ckv, wkv_ref[...], preferred_element_type=F32)
    cos, sin = cos_ref[...], sin_ref[...]
    gq, gk = gq_ref[...], gk_ref[...]
    kpe_ss = jnp.sum(kpe * kpe, axis=-1, keepdims=True)

    def rope64(y):
        return y * cos + (pltpu.roll(y, 32, 1) + pltpu.roll(y, 96, 1)) * sin

    for h in range(MLA_HEADS):
        a = slice(h * 128, (h + 1) * 128)
        n0 = h * MLA_HEAD_PAD
        qn_h, qr_h = qn[:, a], qr[:, a]
        ms = (jnp.sum(qn_h * qn_h, axis=-1, keepdims=True)
              + jnp.sum(qr_h * qr_h, axis=-1, keepdims=True)) * (1.0 / MLA_QK)
        r = lax.rsqrt(ms + EPS)
        q_ref[:, n0:n0 + 128] = (qn_h * r * gq[:, :128]).astype(BF16)
        q_ref[:, n0 + 128:n0 + 256] = rope64(qr_h * r * gq[:, 128:]).astype(BF16)
        kn_h = kv[:, n0:n0 + 128]
        ms = (jnp.sum(kn_h * kn_h, axis=-1, keepdims=True) + kpe_ss) * (1.0 / MLA_QK)
        r = lax.rsqrt(ms + EPS)
        k_ref[:, n0:n0 + 128] = (kn_h * r * gk[:, :128]).astype(BF16)
        k_ref[:, n0 + 128:n0 + 256] = rope64(kpe * r * gk[:, 128:]).astype(BF16)
        v_ref[:, n0:n0 + 128] = kv[:, n0 + 128:n0 + 256].astype(BF16)
        v_ref[:, n0 + 128:n0 + 256] = jnp.ones((kv.shape[0], 128), BF16)


def mla_prep(lat, gcq, gckv, wqn, wqr, wkv, gq, gk, cos, sin, tm=512):
    T = lat.shape[0]
    full = lambda a: pl.BlockSpec(a.shape, lambda i: (0,) * a.ndim)
    rows = lambda w: pl.BlockSpec((tm, w), lambda i: (i, 0))
    return pl.pallas_call(
        _mla_prep_kernel,
        grid=(T // tm,),
        in_specs=[rows(lat.shape[1]), full(gcq), full(gckv), full(wqn), full(wqr), full(wkv),
                  full(gq), full(gk), rows(128), rows(128)],
        out_specs=[rows(MLA_HEADS * MLA_HEAD_PAD)] * 3,
        out_shape=[jax.ShapeDtypeStruct((T, MLA_HEADS * MLA_HEAD_PAD), BF16)] * 3,
        compiler_params=_cparams("parallel"),
        name="mla_prep",
    )(lat, gcq, gckv, wqn, wqr, wkv, gq, gk, cos, sin)


def _flash_kernel(q_ref, k_ref, v_ref, o_ref, acc_ref, m_ref, *, t):
    qi = pl.program_id(2)
    q = q_ref[...]
    nt = (((1,), (1,)), ((), ()))
    chunks = [slice(c * 128, (c + 1) * 128) for c in range(t // 128)]

    def probs(s, m):
        return jnp.concatenate([jnp.exp2(s[:, c] - m) for c in chunks], axis=1).astype(BF16)

    kd = pl.multiple_of(qi * t, t)
    s = lax.dot_general(q, k_ref[pl.ds(kd, t), :], nt, preferred_element_type=F32)
    r = lax.broadcasted_iota(jnp.int32, (t, t), 0)
    c = lax.broadcasted_iota(jnp.int32, (t, t), 1)
    s = jnp.where(c <= r, s, NEG_INF)
    m0 = jnp.broadcast_to(jnp.max(s, axis=-1, keepdims=True), (t, 128))
    m_ref[...] = m0
    acc_ref[...] = jnp.dot(probs(s, m0), v_ref[pl.ds(kd, t), :], preferred_element_type=F32)

    def rescaled(ks):
        s2 = lax.dot_general(q_ref[...], k_ref[pl.ds(ks, t), :], nt, preferred_element_type=F32)
        m_old = m_ref[...]
        m_new = jnp.maximum(m_old, jnp.max(s2, axis=-1, keepdims=True))
        alpha = jnp.exp2(m_old - m_new)
        pv2 = jnp.dot(probs(s2, m_new), v_ref[pl.ds(ks, t), :], preferred_element_type=F32)
        acc_ref[...] = jnp.concatenate([alpha, alpha], axis=1) * acc_ref[...] + pv2
        m_ref[...] = m_new

    def blocks(starts):
        m = m_ref[...]
        over, pv = None, None
        for ks in starts:
            s = lax.dot_general(q, k_ref[pl.ds(ks, t), :], nt, preferred_element_type=F32)
            d = [s[:, c] - m for c in chunks]
            over = functools.reduce(jnp.maximum, d if over is None else d + [over])
            p = jnp.concatenate([jnp.exp2(dc) for dc in d], axis=1).astype(BF16)
            part = jnp.dot(p, v_ref[pl.ds(ks, t), :], preferred_element_type=F32)
            pv = part if pv is None else pv + part
        ok = jnp.max(over) <= FLASH_SLACK

        @pl.when(ok)
        def _():
            acc_ref[...] += pv

        @pl.when(jnp.logical_not(ok))
        def _():
            for ks in starts:
                rescaled(ks)

    def pair(kp, carry):
        ks = pl.multiple_of(kp * (2 * t), 2 * t)
        blocks([ks, pl.multiple_of(ks + t, t)])
        return carry

    lax.fori_loop(0, qi // 2, pair, 0)

    @pl.when(qi % 2 == 1)
    def _():
        blocks([pl.multiple_of((qi - 1) * t, t)])

    o_ref[...] = (acc_ref[:, :MLA_V] / acc_ref[:, MLA_V:]).astype(o_ref.dtype)


def mla_flash(q, k, v1, B, S, t=512):
    T = B * S
    nq = S // t
    kern = functools.partial(_flash_kernel, t=t)
    return pl.pallas_call(
        kern,
        grid=(B, MLA_HEADS, nq),
        in_specs=[pl.BlockSpec((t, MLA_HEAD_PAD), lambda b, h, i: (b * nq + i, h)),
                  pl.BlockSpec((S, MLA_HEAD_PAD), lambda b, h, i: (b, h)),
                  pl.BlockSpec((S, 2 * MLA_V), lambda b, h, i: (b, h))],
        out_specs=pl.BlockSpec((t, MLA_V), lambda b, h, i: (b * nq + i, h)),
        out_shape=jax.ShapeDtypeStruct((T, MLA_HEADS * MLA_V), BF16),
        scratch_shapes=[pltpu.VMEM((t, 2 * MLA_V), F32), pltpu.VMEM((t, 128), F32)],
        compiler_params=_cparams("parallel", "parallel", "arbitrary"),
        name="mla_flash",
    )(q, k, v1)


def _merge_kernel(x_ref, g_ref, osb_ref, odil_ref, omla_ref, wg0_ref, wg1_ref, wg2_ref,
                  wsb_ref, wdil_ref, wmla_ref, wout_ref, o_ref, u_ref):
    j = pl.program_id(1)

    @pl.when(j == 0)
    def _():
        x = x_ref[...]
        u_ref[...] = _rms(x, g_ref[...]).astype(BF16)
        o_ref[...] = x

    u = u_ref[...]

    def branch(o_b, wg, wb):
        gate = jax.nn.sigmoid(jnp.dot(u, wg[...], preferred_element_type=F32))
        return gate * jnp.dot(o_b[...], wb[...], preferred_element_type=F32)

    merged = (branch(osb_ref, wg0_ref, wsb_ref) + branch(odil_ref, wg1_ref, wdil_ref)
              + branch(omla_ref, wg2_ref, wmla_ref))
    o_ref[...] += jnp.dot(merged.astype(BF16), wout_ref[...], preferred_element_type=F32)


def merge_out(x, g, osb, odil, omla, wg, wsb, wdil, wmla, wout, tm=512, tn=256):
    T, D = x.shape
    nj = D // tn
    rows = lambda w: pl.BlockSpec((tm, w), lambda i, j: (i, 0))
    cols = lambda k: pl.BlockSpec((k, tn), lambda i, j: (0, j))
    gcol = lambda b: pl.BlockSpec((D, tn), lambda i, j: (0, b * nj + j))
    return pl.pallas_call(
        _merge_kernel,
        grid=(T // tm, nj),
        in_specs=[rows(D), pl.BlockSpec((1, D), lambda i, j: (0, 0)),
                  rows(osb.shape[1]), rows(odil.shape[1]), rows(omla.shape[1]),
                  gcol(0), gcol(1), gcol(2),
                  cols(wsb.shape[0]), cols(wdil.shape[0]), cols(wmla.shape[0]),
                  pl.BlockSpec((tn, D), lambda i, j: (j, 0))],
        out_specs=rows(D),
        out_shape=jax.ShapeDtypeStruct((T, D), F32),
        scratch_shapes=[pltpu.VMEM((tm, D), BF16)],
        compiler_params=_cparams("parallel", "arbitrary"),
        name="merge_out",
    )(x, g[None, :], osb, odil, omla, wg, wg, wg, wsb, wdil, wmla, wout)


def _top_desc(s, k):
    vals = []
    for _ in range(k):
        mx = jnp.max(s, axis=0, keepdims=True)
        vals.append(mx)
        s = jnp.where(s == mx, -jnp.inf, s)
    return jnp.concatenate(vals, axis=0)


def _peer_front_kernel(h_ref, g_ref, wq_ref, keys_ref, xT_ref, r2_ref, e2_ref, n1_ref, e1_ref):
    hn = _rms(h_ref[...], g_ref[...])
    xT_ref[...] = hn.T.astype(BF16)
    q = jnp.dot(hn.astype(BF16), wq_ref[...], preferred_element_type=F32).astype(BF16)
    nt = (((1,), (1,)), ((), ()))
    for h in range(PEER_HEADS):
        c0 = h * 2 * PEER_HALF
        s1 = lax.dot_general(keys_ref[0], q[:, c0:c0 + PEER_HALF], nt, preferred_element_type=F32)
        s2 = lax.dot_general(keys_ref[1], q[:, c0 + PEER_HALF:c0 + 2 * PEER_HALF], nt,
                             preferred_element_type=F32)
        t1 = _top_desc(s1, PEER_TOPK)
        t2 = _top_desc(s2, PEER_TOPK)
        cand = jnp.concatenate([t1[a:a + 1] + t2 for a in range(PEER_TOPK)], axis=0)
        best = _top_desc(cand, PEER_TOPK)
        tau = best[PEER_TOPK - 1:PEER_TOPK]
        z = jnp.sum(jnp.where(cand >= tau, jnp.exp(cand - best[0:1]), 0.0), axis=0, keepdims=True)
        rank2 = jnp.zeros_like(s2)
        n1 = jnp.zeros_like(s1)
        for b in range(PEER_TOPK):
            tb = t2[b:b + 1]
            rank2 = rank2 + jnp.where(s2 < tb, 1.0, 0.0)
            n1 = n1 + jnp.where(s1 + tb >= tau, 1.0, 0.0)
        r2_ref[h] = rank2.astype(BF16)
        e2_ref[h] = jnp.exp(s2 - t2[0:1]).astype(BF16)
        n1_ref[h] = n1
        e1_ref[h] = jnp.exp(s1 - t1[0:1]) / z


def peer_front(h, g, wq, keys, tm=256):
    T, D = h.shape
    tok = lambda: pl.BlockSpec((PEER_HEADS, PEER_NKEYS, tm), lambda i: (0, 0, i))
    tok_shape = jax.ShapeDtypeStruct((PEER_HEADS, PEER_NKEYS, T), F32)
    tok_half = jax.ShapeDtypeStruct((PEER_HEADS, PEER_NKEYS, T), BF16)
    return pl.pallas_call(
        _peer_front_kernel,
        grid=(T // tm,),
        in_specs=[pl.BlockSpec((tm, D), lambda i: (i, 0)),
                  pl.BlockSpec((1, D), lambda i: (0, 0)),
                  pl.BlockSpec(wq.shape, lambda i: (0, 0)),
                  pl.BlockSpec(keys.shape, lambda i: (0, 0, 0))],
        out_specs=[pl.BlockSpec((D, tm), lambda i: (0, i)), tok(), tok(), tok(), tok()],
        out_shape=[jax.ShapeDtypeStruct((D, T), BF16), tok_half, tok_half, tok_shape, tok_shape],
        compiler_params=_cparams("parallel"),
        name="peer_front",
    )(h, g[None, :], wq, keys)


def _transpose_cast_kernel(x_ref, o_ref):
    o_ref[...] = x_ref[...].T.astype(o_ref.dtype)


def transpose_cast(x, dtype, tr=512):
    R, C = x.shape
    return pl.pallas_call(
        _transpose_cast_kernel,
        grid=(R // tr,),
        in_specs=[pl.BlockSpec((tr, C), lambda i: (i, 0))],
        out_specs=pl.BlockSpec((C, tr), lambda i: (0, i)),
        out_shape=jax.ShapeDtypeStruct((C, R), dtype),
        compiler_params=_cparams("parallel"),
        name="transpose_cast",
    )(x)


PEER_CHUNKS = 4
PEER_HALF_ROWS = PEER_CHUNKS * PEER_NKEYS


def _peer_dense_kernel(xT_ref, u0_ref, ub_ref, ua_ref, vT_ref, r2_ref, e2_ref, n1_ref, e1_ref, h_ref, o_ref,
                       acc_ref, aa_ref, ab_ref):
    e = pl.program_id(1)

    @pl.when(e == 0)
    def _():
        acc_ref[...] = jnp.zeros_like(acc_ref)
        aa_ref[...] = jnp.dot(u0_ref[...], xT_ref[...], preferred_element_type=F32)

    def hidden(a_ref, k0):
        parts = []
        for c in range(PEER_CHUNKS):
            k = k0 + c
            gate = None
            for h in range(PEER_HEADS):
                n1 = n1_ref[h, k:k + 1, :].astype(BF16)
                e1 = e1_ref[h, k:k + 1, :].astype(BF16)
                w = jnp.where(r2_ref[h] < n1, e2_ref[h] * e1, jnp.zeros((), BF16))
                gate = w if gate is None else gate + w
            a = a_ref[c * PEER_NKEYS:(c + 1) * PEER_NKEYS, :]
            gelu = 0.5 * a * (1.0 + lax.erf(a * (2.0 ** -0.5)))
            parts.append(gelu.astype(BF16) * gate)
        return jnp.concatenate(parts, axis=0)

    xT = xT_ref[...]
    ab_ref[...] = jnp.dot(ub_ref[...], xT, preferred_element_type=F32)
    h_a = hidden(aa_ref, 0)
    acc_ref[...] += jnp.dot(vT_ref[:, :PEER_HALF_ROWS], h_a, preferred_element_type=F32)
    aa_ref[...] = jnp.dot(ua_ref[...], xT, preferred_element_type=F32)
    h_b = hidden(ab_ref, PEER_CHUNKS)
    acc_ref[...] += jnp.dot(vT_ref[:, PEER_HALF_ROWS:], h_b, preferred_element_type=F32)

    @pl.when(e == pl.num_programs(1) - 1)
    def _():
        o_ref[...] = h_ref[...] + acc_ref[...].T


def peer_dense(xT, u, vT, r2, e2, n1, e1, h, tm=512):
    D, T = xT.shape
    E = u.shape[0]
    hr = PEER_HALF_ROWS
    ne = E // (2 * hr)
    ni = 2 * PEER_CHUNKS
    tok = lambda: pl.BlockSpec((PEER_HEADS, PEER_NKEYS, tm), lambda i, e: (0, 0, i))
    key = lambda: pl.BlockSpec((PEER_HEADS, ni, tm), lambda i, e: (0, e, i))
    return pl.pallas_call(
        _peer_dense_kernel,
        grid=(T // tm, ne),
        in_specs=[pl.BlockSpec((D, tm), lambda i, e: (0, i)),
                  pl.BlockSpec((hr, D), lambda i, e: (0, 0)),
                  pl.BlockSpec((hr, D), lambda i, e: (2 * e + 1, 0)),
                  pl.BlockSpec((hr, D), lambda i, e: (jnp.minimum(2 * e + 2, 2 * ne - 1), 0)),
                  pl.BlockSpec((D, 2 * hr), lambda i, e: (0, e)),
                  tok(), tok(), key(), key(),
                  pl.BlockSpec((tm, D), lambda i, e: (i, 0))],
        out_specs=pl.BlockSpec((tm, D), lambda i, e: (i, 0)),
        out_shape=jax.ShapeDtypeStruct((T, D), F32),
        scratch_shapes=[pltpu.VMEM((D, tm), F32), pltpu.VMEM((hr, tm), F32), pltpu.VMEM((hr, tm), F32)],
        compiler_params=_cparams("parallel", "arbitrary"),
        name="peer_dense",
    )(xT, u, u, u, vT, r2, e2, n1, e1, h)


def _pad_cols(w, n):
    return jnp.pad(w, ((0, 0), (0, n - w.shape[1])))


def _layer(x, B, S, tabs, norm_mix, w_in, dil_q_norm, dil_k_norm, mla_cq_norm, mla_ckv_norm, mla_w_uq,
           mla_w_ukv, mla_q_norm, mla_k_norm, w_branch_sb, w_branch_dil, w_branch_mla, w_out, norm_ffn,
           peer_w_query, peer_sub_keys, peer_u, peer_v):
    cos128, sin128, cos64, sin64 = tabs
    o_sbq, o_dq, o_dv = 0, 3 * SB_WIDTH, 3 * SB_WIDTH + 2 * DIL_WIDTH
    o_cq = 3 * SB_WIDTH + 3 * DIL_WIDTH

    w_a = jnp.concatenate([w_in[:, o_sbq:o_dq], w_in[:, o_dv:o_cq]], axis=1).astype(BF16)
    w_b = w_in[:, o_dq:o_dv].astype(BF16)
    w_c = _pad_cols(w_in[:, o_cq:MIX_COLS], 896).astype(BF16)
    w_g = w_in[:, MIX_COLS:].astype(BF16)
    p_a = norm_matmul(x, norm_mix, w_a, BF16, tn=768, name="in_proj_a")
    p_b = norm_matmul(x, norm_mix, w_b, F32, tn=768, name="in_proj_b")
    p_c = norm_matmul(x, norm_mix, w_c, F32, tn=896, name="in_proj_c")

    o_sb = sb_attention(p_a, B, S, col0=0)

    gains = jnp.concatenate([jnp.broadcast_to(dil_q_norm, (DIL_HEADS, HEAD_DIM)),
                             jnp.broadcast_to(dil_k_norm, (DIL_HEADS, HEAD_DIM))])[:, None, :]
    dqk = dil_prep(p_b, gains, cos128, sin128)
    outs, lses = zip(*[dil_attention(dqk, p_a, B, S, g, v_col0=3 * SB_WIDTH // DIL_OUT_WIDTH)
                       for g in range(len(DIL_PATTERNS))])
    o_dil = dil_merge(outs, lses)

    uq = mla_w_uq.reshape(MLA_Q_RANK, MLA_HEADS, MLA_QK)
    wqn = uq[:, :, :MLA_NOPE].reshape(MLA_Q_RANK, MLA_HEADS * MLA_NOPE).astype(BF16)
    wqr = jnp.pad(uq[:, :, MLA_NOPE:], ((0, 0), (0, 0), (0, 128 - MLA_ROPE)))
    wqr = wqr.reshape(MLA_Q_RANK, MLA_HEADS * 128).astype(BF16)
    gpad = lambda gn: jnp.pad(gn, (0, MLA_HEAD_PAD - MLA_QK))[None, :]
    q, k, v = mla_prep(p_c, mla_cq_norm[None, :], mla_ckv_norm[None, :], wqn, wqr, mla_w_ukv.astype(BF16),
                       gpad(mla_q_norm * (MLA_QK ** -0.5 * LOG2E)), gpad(mla_k_norm), cos64, sin64)
    o_mla = mla_flash(q, k, v, B, S)

    h = merge_out(x, norm_mix, o_sb, o_dil, o_mla, w_g, w_branch_sb.astype(BF16), w_branch_dil.astype(BF16),
                  w_branch_mla.astype(BF16), w_out.astype(BF16))

    xT, r2, e2, n1, e1 = peer_front(h, norm_ffn, peer_w_query.astype(BF16), peer_sub_keys.astype(BF16))
    return peer_dense(xT, peer_u.astype(BF16), transpose_cast(peer_v, BF16), r2, e2, n1, e1, h)


def kernel(x, positions, norm_mix, w_in, dil_q_norm, dil_k_norm, mla_cq_norm, mla_ckv_norm, mla_w_uq, mla_w_ukv,
           mla_q_norm, mla_k_norm, w_branch_sb, w_branch_dil, w_branch_mla, w_out, norm_ffn, peer_w_query,
           peer_sub_keys, peer_u, peer_v):
    B, S, D = x.shape
    pos_col = positions.reshape(B * S, 1)
    tabs = rope_tables(pos_col, HEAD_DIM // 2) + rope_tables(pos_col, MLA_ROPE // 2)
    xf = x.reshape(B * S, D)
    for l in range(w_in.shape[0]):
        xf = _layer(xf, B, S, tabs, norm_mix[l], w_in[l], dil_q_norm[l], dil_k_norm[l], mla_cq_norm[l],
                    mla_ckv_norm[l], mla_w_uq[l], mla_w_ukv[l], mla_q_norm[l], mla_k_norm[l], w_branch_sb[l],
                    w_branch_dil[l], w_branch_mla[l], w_out[l], norm_ffn[l], peer_w_query[l],
                    peer_sub_keys[l], peer_u[l], peer_v[l])
    return xf.reshape(B, S, D)
```

```python
import functools

import numpy as np
import jax
import jax.numpy as jnp
from jax import lax
from jax.experimental import pallas as pl
from jax.experimental.pallas import tpu as pltpu

F32 = jnp.float32
BF16 = jnp.bfloat16

D_MODEL = 2048
HEAD_DIM = 128
ROPE_THETA = 10000.0
EPS = 1e-6
NEG_INF = -1e30
LOG2E = 1.4426950408889634

SB_HEADS = 4
SB_WIDTH = SB_HEADS * HEAD_DIM
DIL_PATTERNS = ((128, 1), (512, 4), (2048, 16))
DIL_HEADS_PER_GROUP = 2
DIL_HEADS = DIL_HEADS_PER_GROUP * len(DIL_PATTERNS)
DIL_WIDTH = DIL_HEADS * HEAD_DIM
DIL_OUT_WIDTH = DIL_HEADS_PER_GROUP * HEAD_DIM
DIL_BLOCK = 128
MLA_HEADS = 6
MLA_NOPE = 128
MLA_ROPE = 64
MLA_V = 128
MLA_QK = MLA_NOPE + MLA_ROPE
MLA_Q_RANK = 512
MLA_KV_RANK = 256
MLA_HEAD_PAD = 256
N_BRANCH = 3
MIX_COLS = 3 * SB_WIDTH + 3 * DIL_WIDTH + MLA_Q_RANK + MLA_KV_RANK + MLA_ROPE
PEER_HEADS = 8
PEER_NKEYS = 128
PEER_EXPERTS = PEER_NKEYS * PEER_NKEYS
PEER_HALF = 128
PEER_TOPK = 16

SB_EXIT = -104.0

FLASH_SLACK = 60.0

VMEM_LIMIT = 56 * 1024 * 1024


def _cparams(*sem):
    return pltpu.CompilerParams(dimension_semantics=sem, vmem_limit_bytes=VMEM_LIMIT)


def _rms(x, g):
    return x * lax.rsqrt(jnp.mean(x * x, axis=-1, keepdims=True) + EPS) * g


def _rope_table_kernel(pos_ref, inv_ref, sign_ref, cos_ref, sin_ref):
    ang = pos_ref[...].astype(F32) * inv_ref[...]
    cos_ref[...] = jnp.cos(ang)
    sin_ref[...] = jnp.sin(ang) * sign_ref[...]


def rope_tables(pos_col, half, tm=2048):
    T = pos_col.shape[0]
    inv_h = ROPE_THETA ** (-jnp.arange(half, dtype=F32) / half)
    pad = jnp.zeros((128 - 2 * half,), F32)
    inv = jnp.concatenate([inv_h, inv_h, pad])[None, :]
    sign = jnp.concatenate([-jnp.ones((half,), F32), jnp.ones((half,), F32), pad])[None, :]
    row = pl.BlockSpec((1, 128), lambda i: (0, 0))
    cos, sin = pl.pallas_call(
        _rope_table_kernel,
        grid=(T // tm,),
        in_specs=[pl.BlockSpec((tm, 1), lambda i: (i, 0)), row, row],
        out_specs=[pl.BlockSpec((tm, 128), lambda i: (i, 0))] * 2,
        out_shape=[jax.ShapeDtypeStruct((T, 128), F32)] * 2,
        compiler_params=_cparams("parallel"),
        name="rope_tables",
    )(pos_col, inv, sign)
    return cos, sin


def _norm_matmul_kernel(x_ref, g_ref, w_ref, o_ref, u_ref):
    @pl.when(pl.program_id(1) == 0)
    def _():
        u_ref[...] = _rms(x_ref[...], g_ref[...]).astype(BF16)

    o_ref[...] = jnp.dot(u_ref[...], w_ref[...], preferred_element_type=F32).astype(o_ref.dtype)


def norm_matmul(x, g, w, out_dtype, tn, col0=0, n_cols=None, tm=512, name="norm_matmul"):
    T, D = x.shape
    N = w.shape[1] if n_cols is None else n_cols
    j0 = col0 // tn
    return pl.pallas_call(
        _norm_matmul_kernel,
        grid=(T // tm, N // tn),
        in_specs=[pl.BlockSpec((tm, D), lambda i, j: (i, 0)),
                  pl.BlockSpec((1, D), lambda i, j: (0, 0)),
                  pl.BlockSpec((D, tn), lambda i, j: (0, j0 + j))],
        out_specs=pl.BlockSpec((tm, tn), lambda i, j: (i, j)),
        out_shape=jax.ShapeDtypeStruct((T, N), out_dtype),
        scratch_shapes=[pltpu.VMEM((tm, D), BF16)],
        compiler_params=_cparams("parallel", "arbitrary"),
        name=name,
    )(x, g[None, :], w)


def _sb_kernel(q_ref, k_ref, v_ref, tri_ref, o_ref, acc_ref, carry_ref, *, tq, tk, scale):
    q0 = pl.program_id(1) * tq
    acc_ref[...] = jnp.zeros_like(acc_ref)
    carry_ref[...] = jnp.zeros_like(carry_ref)
    tri = tri_ref[...]
    row = q0 + lax.broadcasted_iota(jnp.int32, (tq, tk), 0)
    lane = lax.broadcasted_iota(jnp.int32, (tq, tk), 1)

    def cond(c):
        kb, top = c
        return jnp.logical_and(kb >= 0, top > SB_EXIT)

    def body(c):
        kb, _ = c
        ks = pl.multiple_of(kb * tk, tk)
        mask = (ks + lane) < row
        tops = []
        for h in range(SB_HEADS):
            sl = slice(h * HEAD_DIM, (h + 1) * HEAD_DIM)
            k = k_ref[pl.ds(ks, tk), sl]
            v = v_ref[pl.ds(ks, tk), sl]
            z = lax.dot_general(q_ref[:, sl], k, (((1,), (1,)), ((), ())), preferred_element_type=F32) * scale
            lk = -(jnp.maximum(z, 0.0) + jnp.log1p(jnp.exp(-jnp.abs(z))))
            lk = jnp.where(mask, lk, 0.0)
            hi = lk.astype(BF16)
            lo = (lk - hi.astype(F32)).astype(BF16)
            loc = (jnp.dot(hi, tri, preferred_element_type=F32)
                   + jnp.dot(lo, tri, preferred_element_type=F32))
            carry = carry_ref[:, sl]
            after = carry + loc[:, :tk]
            w = jnp.where(mask, jnp.exp(z + lk + after), 0.0)
            acc_ref[:, sl] += jnp.dot(w.astype(BF16), v, preferred_element_type=F32)
            new_carry = carry + loc[:, tk:]
            carry_ref[:, sl] = new_carry
            tops.append(jnp.max(new_carry))
        return kb - 1, functools.reduce(jnp.maximum, tops)

    lax.while_loop(cond, body, ((q0 + tq) // tk - 1, jnp.float32(0.0)))
    o_ref[...] = acc_ref[...].astype(o_ref.dtype)


def sb_attention(qkv, B, S, tq=256, tk=HEAD_DIM):
    T = B * S
    nq = S // tq
    tri = np.concatenate([np.triu(np.ones((tk, tk), np.float32), 1).T, np.ones((tk, tk), np.float32)], axis=1)
    tri = jnp.asarray(tri, BF16)
    kern = functools.partial(_sb_kernel, tq=tq, tk=tk, scale=HEAD_DIM ** -0.5)
    return pl.pallas_call(
        kern,
        grid=(B, nq),
        in_specs=[pl.BlockSpec((tq, SB_WIDTH), lambda b, i: (b * nq + i, 0)),
                  pl.BlockSpec((S, SB_WIDTH), lambda b, i: (b, 1)),
                  pl.BlockSpec((S, SB_WIDTH), lambda b, i: (b, 2)),
                  pl.BlockSpec((tk, 2 * tk), lambda b, i: (0, 0))],
        out_specs=pl.BlockSpec((tq, SB_WIDTH), lambda b, i: (b * nq + i, 0)),
        out_shape=jax.ShapeDtypeStruct((T, SB_WIDTH), BF16),
        scratch_shapes=[pltpu.VMEM((tq, SB_WIDTH), F32), pltpu.VMEM((tq, SB_WIDTH), F32)],
        compiler_params=_cparams("parallel", "arbitrary"),
        name="sb_attention",
    )(qkv, qkv, qkv, tri)


DIL_CHUNK = 2048


def _dil_group(g, d, prev_chunk_valid, q_refs, kc_refs, kp_refs, vc_refs, vp_refs, cosc_ref, sinc_ref, cosp_ref,
               sinp_ref, gq, gk, oacc_ref, lse_ref):
    blk = DIL_BLOCK
    nblk = DIL_CHUNK // (blk * d)
    n_back = DIL_PATTERNS[g][0] // d
    scale = HEAD_DIM ** -0.5
    qi = lax.broadcasted_iota(jnp.int32, (blk, blk), 0)
    kj = lax.broadcasted_iota(jnp.int32, (blk, blk), 1)
    band_prev = blk + qi - kj <= n_back
    mask_cur = kj <= qi
    nt = (((1,), (1,)), ((), ()))

    def rows(m, r):
        return pl.ds(m * blk * d + r, blk, stride=d) if d > 1 else pl.ds(m * blk, blk)

    def prep(x, gain, cos, sin):
        y = _rms(x, gain)
        return (y * cos + pltpu.roll(y, HEAD_DIM // 2, 1) * sin).astype(BF16)

    for m in range(nblk):
        for r in range(d):
            cur = rows(m, r)
            cos_c, sin_c = cosc_ref[cur, :], sinc_ref[cur, :]
            if m > 0:
                prv = rows(m - 1, r)
                kp_src, vp_src, cos_p, sin_p = kc_refs, vc_refs, cosc_ref[prv, :], sinc_ref[prv, :]
                mask_prev = band_prev
            else:
                prv = rows(nblk - 1, r)
                kp_src, vp_src, cos_p, sin_p = kp_refs, vp_refs, cosp_ref[prv, :], sinp_ref[prv, :]
                mask_prev = jnp.logical_and(band_prev, prev_chunk_valid)
            for h in range(DIL_HEADS_PER_GROUP):
                q = prep(q_refs[h][cur, :], gq, cos_c, sin_c)
                kc = prep(kc_refs[h][cur, :], gk, cos_c, sin_c)
                kp = prep(kp_src[h][prv, :], gk, cos_p, sin_p)
                sp = lax.dot_general(q, kp, nt, preferred_element_type=F32) * scale
                sc = lax.dot_general(q, kc, nt, preferred_element_type=F32) * scale
                sp = jnp.where(mask_prev, sp, NEG_INF)
                sc = jnp.where(mask_cur, sc, NEG_INF)
                mx = jnp.maximum(jnp.max(sp, axis=-1, keepdims=True), jnp.max(sc, axis=-1, keepdims=True))
                ep = jnp.exp(sp - mx)
                ec = jnp.exp(sc - mx)
                den = jnp.sum(ep, axis=-1, keepdims=True) + jnp.sum(ec, axis=-1, keepdims=True)
                v_prev = vp_src[h][prv, :].astype(BF16)
                v_cur = vc_refs[h][cur, :].astype(BF16)
                oacc_ref[g, h, cur, :] = (jnp.dot((ep / den).astype(BF16), v_prev, preferred_element_type=F32)
                                          + jnp.dot((ec / den).astype(BF16), v_cur, preferred_element_type=F32))
                lse_ref[g, h, cur, :] = jnp.broadcast_to(mx + jnp.log(den), (blk, HEAD_DIM))


def _dil_kernel(*refs):
    nh = DIL_HEADS_PER_GROUP
    q_refs, kc_refs, kp_refs, vc_refs, vp_refs = (refs[i * nh:(i + 1) * nh] for i in range(5))
    cosc_ref, sinc_ref, cosp_ref, sinp_ref, gq_ref, gk_ref, o_ref, oacc_ref, lse_ref = refs[5 * nh:]
    prev_chunk_valid = pl.program_id(1) > 0
    g = pl.program_id(2)
    gq, gk = gq_ref[...], gk_ref[...]
    for gi, (_, d) in enumerate(DIL_PATTERNS):
        @pl.when(g == gi)
        def _(gi=gi, d=d):
            _dil_group(gi, d, prev_chunk_valid, q_refs, kc_refs, kp_refs, vc_refs, vp_refs, cosc_ref, sinc_ref,
                       cosp_ref, sinp_ref, gq, gk, oacc_ref, lse_ref)

    @pl.when(g == len(DIL_PATTERNS) - 1)
    def _():
        for h in range(nh):
            a0, a1, a2 = lse_ref[0, h], lse_ref[1, h], lse_ref[2, h]
            mx = jnp.maximum(jnp.maximum(a0, a1), a2)
            e0, e1, e2 = jnp.exp(a0 - mx), jnp.exp(a1 - mx), jnp.exp(a2 - mx)
            den = e0 + e1 + e2
            o_ref[:, h * HEAD_DIM:(h + 1) * HEAD_DIM] = (
                (e0 / den) * oacc_ref[0, h] + (e1 / den) * oacc_ref[1, h] + (e2 / den) * oacc_ref[2, h]
            ).astype(o_ref.dtype)


def dil_attention(p, gq, gk, cos, sin, B, S):
    T = B * S
    nch = S // DIL_CHUNK
    ng, nh = len(DIL_PATTERNS), DIL_HEADS_PER_GROUP
    cur = lambda b, n: b * nch + n
    prev = lambda b, n: b * nch + jnp.maximum(n - 1, 0)

    def heads(row, part):
        return [pl.BlockSpec((DIL_CHUNK, HEAD_DIM),
                             lambda b, n, g, h=h: (row(b, n), part * DIL_HEADS + g * nh + h)) for h in range(nh)]

    tab = lambda row: pl.BlockSpec((DIL_CHUNK, HEAD_DIM), lambda b, n, g: (row(b, n), 0))
    gain = pl.BlockSpec((1, HEAD_DIM), lambda b, n, g: (0, 0))
    return pl.pallas_call(
        _dil_kernel,
        grid=(B, nch, ng),
        in_specs=(heads(cur, 0) + heads(cur, 1) + heads(prev, 1) + heads(cur, 2) + heads(prev, 2)
                  + [tab(cur), tab(cur), tab(prev), tab(prev), gain, gain]),
        out_specs=pl.BlockSpec((DIL_CHUNK, DIL_OUT_WIDTH), lambda b, n, g: (cur(b, n), 0)),
        out_shape=jax.ShapeDtypeStruct((T, DIL_OUT_WIDTH), BF16),
        scratch_shapes=[pltpu.VMEM((ng, nh, DIL_CHUNK, HEAD_DIM), F32),
                        pltpu.VMEM((ng, nh, DIL_CHUNK, HEAD_DIM), F32)],
        compiler_params=_cparams("parallel", "arbitrary", "arbitrary"),
        name="dil_attention",
    )(*([p] * (5 * nh)), cos, sin, cos, sin, gq[None, :], gk[None, :])


def _mla_prep_kernel(lat_ref, gcq_ref, gckv_ref, wqn_ref, wqr_ref, wkv_ref, gq_ref, gk_ref,
                     cos_ref, sin_ref, q_ref, k_ref, v_ref):
    lat = lat_ref[...]
    cq = _rms(lat[:, :MLA_Q_RANK], gcq_ref[...]).astype(BF16)
    ckv = _rms(lat[:, MLA_Q_RANK:MLA_Q_RANK + MLA_KV_RANK], gckv_ref[...]).astype(BF16)
    kpe = lat[:, MLA_Q_RANK + MLA_KV_RANK:]
    qn = jnp.dot(cq, wqn_ref[...], preferred_element_type=F32)
    qr = jnp.dot(cq, wqr_ref[...], preferred_element_type=F32)
    kv = jnp.dot(ckv, wkv_ref[...], preferred_element_type=F32)
    cos, sin = cos_ref[...], sin_ref[...]
    gq, gk = gq_ref[...], gk_ref[...]
    kpe_ss = jnp.sum(kpe * kpe, axis=-1, keepdims=True)

    def rope64(y):
        return y * cos + (pltpu.roll(y, 32, 1) + pltpu.roll(y, 96, 1)) * sin

    for h in range(MLA_HEADS):
        a = slice(h * 128, (h + 1) * 128)
        n0 = h * MLA_HEAD_PAD
        qn_h, qr_h = qn[:, a], qr[:, a]
        ms = (jnp.sum(qn_h * qn_h, axis=-1, keepdims=True)
              + jnp.sum(qr_h * qr_h, axis=-1, keepdims=True)) * (1.0 / MLA_QK)
        r = lax.rsqrt(ms + EPS)
        q_ref[:, n0:n0 + 128] = (qn_h * r * gq[:, :128]).astype(BF16)
        q_ref[:, n0 + 128:n0 + 256] = rope64(qr_h * r * gq[:, 128:]).astype(BF16)
        kn_h = kv[:, n0:n0 + 128]
        ms = (jnp.sum(kn_h * kn_h, axis=-1, keepdims=True) + kpe_ss) * (1.0 / MLA_QK)
        r = lax.rsqrt(ms + EPS)
        k_ref[:, n0:n0 + 128] = (kn_h * r * gk[:, :128]).astype(BF16)
        k_ref[:, n0 + 128:n0 + 256] = rope64(kpe * r * gk[:, 128:]).astype(BF16)
        v_ref[:, n0:n0 + 128] = kv[:, n0 + 128:n0 + 256].astype(BF16)
        v_ref[:, n0 + 128:n0 + 256] = jnp.ones((kv.shape[0], 128), BF16)


def mla_prep(lat, gcq, gckv, wqn, wqr, wkv, gq, gk, cos, sin, tm=512):
    T = lat.shape[0]
    full = lambda a: pl.BlockSpec(a.shape, lambda i: (0,) * a.ndim)
    rows = lambda w: pl.BlockSpec((tm, w), lambda i: (i, 0))
    return pl.pallas_call(
        _mla_prep_kernel,
        grid=(T // tm,),
        in_specs=[rows(lat.shape[1]), full(gcq), full(gckv), full(wqn), full(wqr), full(wkv),
                  full(gq), full(gk), rows(128), rows(128)],
        out_specs=[rows(MLA_HEADS * MLA_HEAD_PAD)] * 3,
        out_shape=[jax.ShapeDtypeStruct((T, MLA_HEADS * MLA_HEAD_PAD), BF16)] * 3,
        compiler_params=_cparams("parallel"),
        name="mla_prep",
    )(lat, gcq, gckv, wqn, wqr, wkv, gq, gk, cos, sin)


def _flash_kernel(q_ref, k_ref, v_ref, o_ref, acc_ref, m_ref, *, t):
    qi = pl.program_id(2)
    q = q_ref[...]
    nt = (((1,), (1,)), ((), ()))
    chunks = [slice(c * 128, (c + 1) * 128) for c in range(t // 128)]

    def probs(s, m):
        return jnp.concatenate([jnp.exp2(s[:, c] - m) for c in chunks], axis=1).astype(BF16)

    kd = pl.multiple_of(qi * t, t)
    s = lax.dot_general(q, k_ref[pl.ds(kd, t), :], nt, preferred_element_type=F32)
    r = lax.broadcasted_iota(jnp.int32, (t, t), 0)
    c = lax.broadcasted_iota(jnp.int32, (t, t), 1)
    s = jnp.where(c <= r, s, NEG_INF)
    m0 = jnp.broadcast_to(jnp.max(s, axis=-1, keepdims=True), (t, 128))
    m_ref[...] = m0
    acc_ref[...] = jnp.dot(probs(s, m0), v_ref[pl.ds(kd, t), :], preferred_element_type=F32)

    def rescaled(ks):
        s2 = lax.dot_general(q_ref[...], k_ref[pl.ds(ks, t), :], nt, preferred_element_type=F32)
        m_old = m_ref[...]
        m_new = jnp.maximum(m_old, jnp.max(s2, axis=-1, keepdims=True))
        alpha = jnp.exp2(m_old - m_new)
        pv2 = jnp.dot(probs(s2, m_new), v_ref[pl.ds(ks, t), :], preferred_element_type=F32)
        acc_ref[...] = jnp.concatenate([alpha, alpha], axis=1) * acc_ref[...] + pv2
        m_ref[...] = m_new

    def blocks(starts):
        m = m_ref[...]
        over, pv = None, None
        for ks in starts:
            s = lax.dot_general(q, k_ref[pl.ds(ks, t), :], nt, preferred_element_type=F32)
            d = [s[:, c] - m for c in chunks]
            over = functools.reduce(jnp.maximum, d if over is None else d + [over])
            p = jnp.concatenate([jnp.exp2(dc) for dc in d], axis=1).astype(BF16)
            part = jnp.dot(p, v_ref[pl.ds(ks, t), :], preferred_element_type=F32)
            pv = part if pv is None else pv + part
        ok = jnp.max(over) <= FLASH_SLACK

        @pl.when(ok)
        def _():
            acc_ref[...] += pv

        @pl.when(jnp.logical_not(ok))
        def _():
            for ks in starts:
                rescaled(ks)

    def pair(kp, carry):
        ks = pl.multiple_of(kp * (2 * t), 2 * t)
        blocks([ks, pl.multiple_of(ks + t, t)])
        return carry

    lax.fori_loop(0, qi // 2, pair, 0)

    @pl.when(qi % 2 == 1)
    def _():
        blocks([pl.multiple_of((qi - 1) * t, t)])

    o_ref[...] = (acc_ref[:, :MLA_V] / acc_ref[:, MLA_V:]).astype(o_ref.dtype)


def mla_flash(q, k, v1, B, S, t=512):
    T = B * S
    nq = S // t
    kern = functools.partial(_flash_kernel, t=t)
    return pl.pallas_call(
        kern,
        grid=(B, MLA_HEADS, nq),
        in_specs=[pl.BlockSpec((t, MLA_HEAD_PAD), lambda b, h, i: (b * nq + i, h)),
                  pl.BlockSpec((S, MLA_HEAD_PAD), lambda b, h, i: (b, h)),
                  pl.BlockSpec((S, 2 * MLA_V), lambda b, h, i: (b, h))],
        out_specs=pl.BlockSpec((t, MLA_V), lambda b, h, i: (b * nq + i, h)),
        out_shape=jax.ShapeDtypeStruct((T, MLA_HEADS * MLA_V), BF16),
        scratch_shapes=[pltpu.VMEM((t, 2 * MLA_V), F32), pltpu.VMEM((t, 128), F32)],
        compiler_params=_cparams("parallel", "parallel", "arbitrary"),
        name="mla_flash",
    )(q, k, v1)


def _merge_kernel(x_ref, g_ref, osb_ref, odil_ref, omla_ref, wg0_ref, wg1_ref, wg2_ref,
                  wsb_ref, wdil_ref, wmla_ref, wout_ref, o_ref, u_ref):
    j = pl.program_id(1)

    @pl.when(j == 0)
    def _():
        x = x_ref[...]
        u_ref[...] = _rms(x, g_ref[...]).astype(BF16)
        o_ref[...] = x

    u = u_ref[...]

    def branch(o_b, wg, wb):
        gate = jax.nn.sigmoid(jnp.dot(u, wg[...], preferred_element_type=F32))
        return gate * jnp.dot(o_b[...], wb[...], preferred_element_type=F32)

    merged = (branch(osb_ref, wg0_ref, wsb_ref) + branch(odil_ref, wg1_ref, wdil_ref)
              + branch(omla_ref, wg2_ref, wmla_ref))
    o_ref[...] += jnp.dot(merged.astype(BF16), wout_ref[...], preferred_element_type=F32)


def merge_out(x, g, osb, odil, omla, wg, wsb, wdil, wmla, wout, tm=512, tn=256):
    T, D = x.shape
    nj = D // tn
    rows = lambda w: pl.BlockSpec((tm, w), lambda i, j: (i, 0))
    cols = lambda k: pl.BlockSpec((k, tn), lambda i, j: (0, j))
    gcol = lambda b: pl.BlockSpec((D, tn), lambda i, j: (0, b * nj + j))
    return pl.pallas_call(
        _merge_kernel,
        grid=(T // tm, nj),
        in_specs=[rows(D), pl.BlockSpec((1, D), lambda i, j: (0, 0)),
                  rows(osb.shape[1]), rows(odil.shape[1]), rows(omla.shape[1]),
                  gcol(0), gcol(1), gcol(2),
                  cols(wsb.shape[0]), cols(wdil.shape[0]), cols(wmla.shape[0]),
                  pl.BlockSpec((tn, D), lambda i, j: (j, 0))],
        out_specs=rows(D),
        out_shape=jax.ShapeDtypeStruct((T, D), F32),
        scratch_shapes=[pltpu.VMEM((tm, D), BF16)],
        compiler_params=_cparams("parallel", "arbitrary"),
        name="merge_out",
    )(x, g[None, :], osb, odil, omla, wg, wg, wg, wsb, wdil, wmla, wout)


def _top_desc(s, k, with_rank=False):
    vals = []
    rank = jnp.full_like(s, float(k))
    for r in range(k):
        mx = jnp.max(s, axis=0, keepdims=True)
        vals.append(mx)
        hit = s == mx
        if with_rank:
            rank = jnp.where(hit, float(r), rank)
        s = jnp.where(hit, -jnp.inf, s)
    vals = jnp.concatenate(vals, axis=0)
    return (vals, rank) if with_rank else vals


def _peer_front_kernel(h_ref, g_ref, wq_ref, keys_ref, xT_ref, r2_ref, e2_ref, n1_ref, e1_ref):
    hn = _rms(h_ref[...], g_ref[...])
    xT_ref[...] = hn.T.astype(BF16)
    q = jnp.dot(hn.astype(BF16), wq_ref[...], preferred_element_type=F32).astype(BF16)
    nt = (((1,), (1,)), ((), ()))
    for h in range(PEER_HEADS):
        c0 = h * 2 * PEER_HALF
        s1 = lax.dot_general(keys_ref[0], q[:, c0:c0 + PEER_HALF], nt, preferred_element_type=F32)
        s2 = lax.dot_general(keys_ref[1], q[:, c0 + PEER_HALF:c0 + 2 * PEER_HALF], nt,
                             preferred_element_type=F32)
        t1 = _top_desc(s1, PEER_TOPK)
        t2, rank2 = _top_desc(s2, PEER_TOPK, with_rank=True)
        cand = jnp.concatenate([t1[0:1] + t2] + [t1[a:a + 1] + t2[0:8] for a in range(1, 8)]
                               + [t1[8:16] + t2[0:1]], axis=0)
        best = _top_desc(cand, PEER_TOPK)
        tau = best[PEER_TOPK - 1:PEER_TOPK]
        z = jnp.sum(jnp.where(cand >= tau, jnp.exp(cand - best[0:1]), 0.0), axis=0, keepdims=True)
        n1 = jnp.zeros_like(s1)
        for b in range(PEER_TOPK):
            n1 = n1 + jnp.where(s1 + t2[b:b + 1] >= tau, 1.0, 0.0)
        r2_ref[h] = rank2.astype(BF16)
        e2_ref[h] = jnp.exp(s2 - t2[0:1]).astype(BF16)
        n1_ref[h] = n1
        e1_ref[h] = jnp.exp(s1 - t1[0:1]) / z


def peer_front(h, g, wq, keys, tm=256):
    T, D = h.shape
    tok = lambda: pl.BlockSpec((PEER_HEADS, PEER_NKEYS, tm), lambda i: (0, 0, i))
    tok_shape = jax.ShapeDtypeStruct((PEER_HEADS, PEER_NKEYS, T), F32)
    tok_half = jax.ShapeDtypeStruct((PEER_HEADS, PEER_NKEYS, T), BF16)
    return pl.pallas_call(
        _peer_front_kernel,
        grid=(T // tm,),
        in_specs=[pl.BlockSpec((tm, D), lambda i: (i, 0)),
                  pl.BlockSpec((1, D), lambda i: (0, 0)),
                  pl.BlockSpec(wq.shape, lambda i: (0, 0)),
                  pl.BlockSpec(keys.shape, lambda i: (0, 0, 0))],
        out_specs=[pl.BlockSpec((D, tm), lambda i: (0, i)), tok(), tok(), tok(), tok()],
        out_shape=[jax.ShapeDtypeStruct((D, T), BF16), tok_half, tok_half, tok_shape, tok_shape],
        compiler_params=_cparams("parallel"),
        name="peer_front",
    )(h, g[None, :], wq, keys)


def _transpose_cast_kernel(x_ref, o_ref):
    o_ref[...] = x_ref[...].T.astype(o_ref.dtype)


def transpose_cast(x, dtype, tr=512):
    R, C = x.shape
    return pl.pallas_call(
        _transpose_cast_kernel,
        grid=(R // tr,),
        in_specs=[pl.BlockSpec((tr, C), lambda i: (i, 0))],
        out_specs=pl.BlockSpec((C, tr), lambda i: (0, i)),
        out_shape=jax.ShapeDtypeStruct((C, R), dtype),
        compiler_params=_cparams("parallel"),
        name="transpose_cast",
    )(x)


PEER_CHUNKS = 4
PEER_HALF_ROWS = PEER_CHUNKS * PEER_NKEYS


def _peer_dense_kernel(xT_ref, u0_ref, ub_ref, ua_ref, vT_ref, r2_ref, e2_ref, n1_ref, e1_ref, h_ref, o_ref,
                       acc_ref, aa_ref, ab_ref):
    e = pl.program_id(1)

    @pl.when(e == 0)
    def _():
        acc_ref[...] = jnp.zeros_like(acc_ref)
        aa_ref[...] = jnp.dot(u0_ref[...], xT_ref[...], preferred_element_type=F32)

    def hidden(a_ref, k0):
        parts = []
        for c in range(PEER_CHUNKS):
            k = k0 + c
            gate = None
            for h in range(PEER_HEADS):
                n1 = n1_ref[h, k:k + 1, :].astype(BF16)
                e1 = e1_ref[h, k:k + 1, :].astype(BF16)
                w = jnp.where(r2_ref[h] < n1, e2_ref[h] * e1, jnp.zeros((), BF16))
                gate = w if gate is None else gate + w
            a = a_ref[c * PEER_NKEYS:(c + 1) * PEER_NKEYS, :]
            gelu = 0.5 * a * (1.0 + lax.erf(a * (2.0 ** -0.5)))
            parts.append(gelu.astype(BF16) * gate)
        return jnp.concatenate(parts, axis=0)

    xT = xT_ref[...]
    ab_ref[...] = jnp.dot(ub_ref[...], xT, preferred_element_type=F32)
    h_a = hidden(aa_ref, 0)
    acc_ref[...] += jnp.dot(vT_ref[:, :PEER_HALF_ROWS], h_a, preferred_element_type=F32)
    aa_ref[...] = jnp.dot(ua_ref[...], xT, preferred_element_type=F32)
    h_b = hidden(ab_ref, PEER_CHUNKS)
    acc_ref[...] += jnp.dot(vT_ref[:, PEER_HALF_ROWS:], h_b, preferred_element_type=F32)

    @pl.when(e == pl.num_programs(1) - 1)
    def _():
        o_ref[...] = h_ref[...] + acc_ref[...].T


def peer_dense(xT, u, vT, r2, e2, n1, e1, h, tm=512):
    D, T = xT.shape
    E = u.shape[0]
    hr = PEER_HALF_ROWS
    ne = E // (2 * hr)
    ni = 2 * PEER_CHUNKS
    tok = lambda: pl.BlockSpec((PEER_HEADS, PEER_NKEYS, tm), lambda i, e: (0, 0, i))
    key = lambda: pl.BlockSpec((PEER_HEADS, ni, tm), lambda i, e: (0, e, i))
    return pl.pallas_call(
        _peer_dense_kernel,
        grid=(T // tm, ne),
        in_specs=[pl.BlockSpec((D, tm), lambda i, e: (0, i)),
                  pl.BlockSpec((hr, D), lambda i, e: (0, 0)),
                  pl.BlockSpec((hr, D), lambda i, e: (2 * e + 1, 0)),
                  pl.BlockSpec((hr, D), lambda i, e: (jnp.minimum(2 * e + 2, 2 * ne - 1), 0)),
                  pl.BlockSpec((D, 2 * hr), lambda i, e: (0, e)),
                  tok(), tok(), key(), key(),
                  pl.BlockSpec((tm, D), lambda i, e: (i, 0))],
        out_specs=pl.BlockSpec((tm, D), lambda i, e: (i, 0)),
        out_shape=jax.ShapeDtypeStruct((T, D), F32),
        scratch_shapes=[pltpu.VMEM((D, tm), F32), pltpu.VMEM((hr, tm), F32), pltpu.VMEM((hr, tm), F32)],
        compiler_params=_cparams("parallel", "arbitrary"),
        name="peer_dense",
    )(xT, u, u, u, vT, r2, e2, n1, e1, h)


def _pad_cols(w, n):
    return jnp.pad(w, ((0, 0), (0, n - w.shape[1])))


def _layer(x, B, S, tabs, norm_mix, w_in, dil_q_norm, dil_k_norm, mla_cq_norm, mla_ckv_norm, mla_w_uq,
           mla_w_ukv, mla_q_norm, mla_k_norm, w_branch_sb, w_branch_dil, w_branch_mla, w_out, norm_ffn,
           peer_w_query, peer_sub_keys, peer_u, peer_v):
    cos128, sin128, cos64, sin64 = tabs
    c_dil, c_lat = 3 * SB_WIDTH, 3 * SB_WIDTH + 3 * DIL_WIDTH

    w_mix = w_in[:, :c_lat].astype(BF16)
    w_c = _pad_cols(w_in[:, c_lat:MIX_COLS], 896).astype(BF16)
    w_g = w_in[:, MIX_COLS:].astype(BF16)
    p_a = norm_matmul(x, norm_mix, w_mix, BF16, tn=768, col0=0, n_cols=c_dil, name="in_proj_a")
    p_b = norm_matmul(x, norm_mix, w_mix, F32, tn=768, col0=c_dil, n_cols=c_lat - c_dil, name="in_proj_b")
    p_c = norm_matmul(x, norm_mix, w_c, F32, tn=896, name="in_proj_c")

    o_sb = sb_attention(p_a, B, S)

    o_dil = dil_attention(p_b, dil_q_norm, dil_k_norm, cos128, sin128, B, S)

    uq = mla_w_uq.reshape(MLA_Q_RANK, MLA_HEADS, MLA_QK)
    wqn = uq[:, :, :MLA_NOPE].reshape(MLA_Q_RANK, MLA_HEADS * MLA_NOPE).astype(BF16)
    wqr = jnp.pad(uq[:, :, MLA_NOPE:], ((0, 0), (0, 0), (0, 128 - MLA_ROPE)))
    wqr = wqr.reshape(MLA_Q_RANK, MLA_HEADS * 128).astype(BF16)
    gpad = lambda gn: jnp.pad(gn, (0, MLA_HEAD_PAD - MLA_QK))[None, :]
    q, k, v = mla_prep(p_c, mla_cq_norm[None, :], mla_ckv_norm[None, :], wqn, wqr, mla_w_ukv.astype(BF16),
                       gpad(mla_q_norm * (MLA_QK ** -0.5 * LOG2E)), gpad(mla_k_norm), cos64, sin64)
    o_mla = mla_flash(q, k, v, B, S)

    h = merge_out(x, norm_mix, o_sb, o_dil, o_mla, w_g, w_branch_sb.astype(BF16), w_branch_dil.astype(BF16),
                  w_branch_mla.astype(BF16), w_out.astype(BF16))

    xT, r2, e2, n1, e1 = peer_front(h, norm_ffn, peer_w_query.astype(BF16), peer_sub_keys.astype(BF16))
    return peer_dense(xT, peer_u.astype(BF16), transpose_cast(peer_v, BF16), r2, e2, n1, e1, h)


def kernel(x, positions, norm_mix, w_in, dil_q_norm, dil_k_norm, mla_cq_norm, mla_ckv_norm, mla_w_uq, mla_w_ukv,
           mla_q_norm, mla_k_norm, w_branch_sb, w_branch_dil, w_branch_mla, w_out, norm_ffn, peer_w_query,
           peer_sub_keys, peer_u, peer_v):
    B, S, D = x.shape
    pos_col = positions.reshape(B * S, 1)
    tabs = rope_tables(pos_col, HEAD_DIM // 2) + rope_tables(pos_col, MLA_ROPE // 2)
    xf = x.reshape(B * S, D)
    for l in range(w_in.shape[0]):
        xf = _layer(xf, B, S, tabs, norm_mix[l], w_in[l], dil_q_norm[l], dil_k_norm[l], mla_cq_norm[l],
                    mla_ckv_norm[l], mla_w_uq[l], mla_w_ukv[l], mla_q_norm[l], mla_k_norm[l], w_branch_sb[l],
                    w_branch_dil[l], w_branch_mla[l], w_out[l], norm_ffn[l], peer_w_query[l],
                    peer_sub_keys[l], peer_u[l], peer_v[l])
    return xf.reshape(B, S, D)
```

```python
import functools

import numpy as np
import jax
import jax.numpy as jnp
from jax import lax
from jax.experimental import pallas as pl
from jax.experimental.pallas import tpu as pltpu

F32 = jnp.float32
BF16 = jnp.bfloat16

D_MODEL = 2048
HEAD_DIM = 128
ROPE_THETA = 10000.0
EPS = 1e-6
NEG_INF = -1e30
LOG2E = 1.4426950408889634

SB_HEADS = 4
SB_WIDTH = SB_HEADS * HEAD_DIM
DIL_PATTERNS = ((128, 1), (512, 4), (2048, 16))
DIL_HEADS_PER_GROUP = 2
DIL_HEADS = DIL_HEADS_PER_GROUP * len(DIL_PATTERNS)
DIL_WIDTH = DIL_HEADS * HEAD_DIM
DIL_OUT_WIDTH = DIL_HEADS_PER_GROUP * HEAD_DIM
DIL_BLOCK = 128
MLA_HEADS = 6
MLA_NOPE = 128
MLA_ROPE = 64
MLA_V = 128
MLA_QK = MLA_NOPE + MLA_ROPE
MLA_Q_RANK = 512
MLA_KV_RANK = 256
MLA_HEAD_PAD = 256
N_BRANCH = 3
MIX_COLS = 3 * SB_WIDTH + 3 * DIL_WIDTH + MLA_Q_RANK + MLA_KV_RANK + MLA_ROPE
PEER_HEADS = 8
PEER_NKEYS = 128
PEER_EXPERTS = PEER_NKEYS * PEER_NKEYS
PEER_HALF = 128
PEER_TOPK = 16

SB_EXIT = -104.0

FLASH_SLACK = 60.0
FLASH_GROUP = 4

VMEM_LIMIT = 56 * 1024 * 1024


def _cparams(*sem):
    return pltpu.CompilerParams(dimension_semantics=sem, vmem_limit_bytes=VMEM_LIMIT)


def _rms(x, g):
    return x * lax.rsqrt(jnp.mean(x * x, axis=-1, keepdims=True) + EPS) * g


def _rope_table_kernel(pos_ref, inv_ref, sign_ref, cos_ref, sin_ref):
    ang = pos_ref[...].astype(F32) * inv_ref[...]
    cos_ref[...] = jnp.cos(ang)
    sin_ref[...] = jnp.sin(ang) * sign_ref[...]


def rope_tables(pos_col, half, tm=2048):
    T = pos_col.shape[0]
    inv_h = ROPE_THETA ** (-jnp.arange(half, dtype=F32) / half)
    pad = jnp.zeros((128 - 2 * half,), F32)
    inv = jnp.concatenate([inv_h, inv_h, pad])[None, :]
    sign = jnp.concatenate([-jnp.ones((half,), F32), jnp.ones((half,), F32), pad])[None, :]
    row = pl.BlockSpec((1, 128), lambda i: (0, 0))
    cos, sin = pl.pallas_call(
        _rope_table_kernel,
        grid=(T // tm,),
        in_specs=[pl.BlockSpec((tm, 1), lambda i: (i, 0)), row, row],
        out_specs=[pl.BlockSpec((tm, 128), lambda i: (i, 0))] * 2,
        out_shape=[jax.ShapeDtypeStruct((T, 128), F32)] * 2,
        compiler_params=_cparams("parallel"),
        name="rope_tables",
    )(pos_col, inv, sign)
    return cos, sin


def _norm_matmul_kernel(x_ref, g_ref, w_ref, o_ref, u_ref):
    @pl.when(pl.program_id(1) == 0)
    def _():
        u_ref[...] = _rms(x_ref[...], g_ref[...]).astype(BF16)

    o_ref[...] = jnp.dot(u_ref[...], w_ref[...], preferred_element_type=F32).astype(o_ref.dtype)


def norm_matmul(x, g, w, out_dtype, tn, col0=0, n_cols=None, tm=512, name="norm_matmul"):
    T, D = x.shape
    N = w.shape[1] if n_cols is None else n_cols
    j0 = col0 // tn
    return pl.pallas_call(
        _norm_matmul_kernel,
        grid=(T // tm, N // tn),
        in_specs=[pl.BlockSpec((tm, D), lambda i, j: (i, 0)),
                  pl.BlockSpec((1, D), lambda i, j: (0, 0)),
                  pl.BlockSpec((D, tn), lambda i, j: (0, j0 + j))],
        out_specs=pl.BlockSpec((tm, tn), lambda i, j: (i, j)),
        out_shape=jax.ShapeDtypeStruct((T, N), out_dtype),
        scratch_shapes=[pltpu.VMEM((tm, D), BF16)],
        compiler_params=_cparams("parallel", "arbitrary"),
        name=name,
    )(x, g[None, :], w)


def _sb_kernel(q_ref, k_ref, v_ref, tri_ref, o_ref, acc_ref, carry_ref, *, tq, tk, scale):
    q0 = pl.program_id(1) * tq
    acc_ref[...] = jnp.zeros_like(acc_ref)
    carry_ref[...] = jnp.zeros_like(carry_ref)
    tri = tri_ref[...]
    row = q0 + lax.broadcasted_iota(jnp.int32, (tq, tk), 0)
    lane = lax.broadcasted_iota(jnp.int32, (tq, tk), 1)

    def cond(c):
        kb, top = c
        return jnp.logical_and(kb >= 0, top > SB_EXIT)

    def body(c):
        kb, _ = c
        ks = pl.multiple_of(kb * tk, tk)
        mask = (ks + lane) < row
        tops = []
        for h in range(SB_HEADS):
            sl = slice(h * HEAD_DIM, (h + 1) * HEAD_DIM)
            k = k_ref[pl.ds(ks, tk), sl]
            v = v_ref[pl.ds(ks, tk), sl]
            z = lax.dot_general(q_ref[:, sl], k, (((1,), (1,)), ((), ())), preferred_element_type=F32) * scale
            lk = -(jnp.maximum(z, 0.0) + jnp.log1p(jnp.exp(-jnp.abs(z))))
            lk = jnp.where(mask, lk, 0.0)
            hi = lk.astype(BF16)
            lo = (lk - hi.astype(F32)).astype(BF16)
            loc = (jnp.dot(hi, tri, preferred_element_type=F32)
                   + jnp.dot(lo, tri, preferred_element_type=F32))
            carry = carry_ref[:, sl]
            after = carry + loc[:, :tk]
            w = jnp.where(mask, jnp.exp(z + lk + after), 0.0)
            acc_ref[:, sl] += jnp.dot(w.astype(BF16), v, preferred_element_type=F32)
            new_carry = carry + loc[:, tk:]
            carry_ref[:, sl] = new_carry
            tops.append(jnp.max(new_carry))
        return kb - 1, functools.reduce(jnp.maximum, tops)

    lax.while_loop(cond, body, ((q0 + tq) // tk - 1, jnp.float32(0.0)))
    o_ref[...] = acc_ref[...].astype(o_ref.dtype)


def sb_attention(qkv, B, S, tq=256, tk=HEAD_DIM):
    T = B * S
    nq = S // tq
    tri = np.concatenate([np.triu(np.ones((tk, tk), np.float32), 1).T, np.ones((tk, tk), np.float32)], axis=1)
    tri = jnp.asarray(tri, BF16)
    kern = functools.partial(_sb_kernel, tq=tq, tk=tk, scale=HEAD_DIM ** -0.5)
    return pl.pallas_call(
        kern,
        grid=(B, nq),
        in_specs=[pl.BlockSpec((tq, SB_WIDTH), lambda b, i: (b * nq + i, 0)),
                  pl.BlockSpec((S, SB_WIDTH), lambda b, i: (b, 1)),
                  pl.BlockSpec((S, SB_WIDTH), lambda b, i: (b, 2)),
                  pl.BlockSpec((tk, 2 * tk), lambda b, i: (0, 0))],
        out_specs=pl.BlockSpec((tq, SB_WIDTH), lambda b, i: (b * nq + i, 0)),
        out_shape=jax.ShapeDtypeStruct((T, SB_WIDTH), BF16),
        scratch_shapes=[pltpu.VMEM((tq, SB_WIDTH), F32), pltpu.VMEM((tq, SB_WIDTH), F32)],
        compiler_params=_cparams("parallel", "arbitrary"),
        name="sb_attention",
    )(qkv, qkv, qkv, tri)


DIL_CHUNK = 2048


def _dil_group(g, d, chunk, q_refs, k_refs, v_refs, cos_ref, sin_ref, gq, gk, kcache_ref, vcache_ref,
               oacc_ref, lse_ref):
    first_chunk = chunk == 0
    par = chunk % 2
    blk = DIL_BLOCK
    nblk = DIL_CHUNK // (blk * d)
    n_back = DIL_PATTERNS[g][0] // d
    scale = HEAD_DIM ** -0.5
    qi = lax.broadcasted_iota(jnp.int32, (blk, 2 * blk), 0)
    kj = lax.broadcasted_iota(jnp.int32, (blk, 2 * blk), 1)
    dist = blk + qi - kj
    band = jnp.logical_and(dist >= 0, dist <= n_back)
    band_first = jnp.logical_and(band, jnp.logical_or(kj >= blk, jnp.logical_not(first_chunk)))
    nt = (((1,), (1,)), ((), ()))
    ones = jnp.ones((2 * blk, HEAD_DIM), BF16)

    @pl.when(first_chunk)
    def _():
        kcache_ref[1, g] = jnp.zeros(kcache_ref.shape[2:], BF16)
        vcache_ref[1, g] = jnp.zeros(vcache_ref.shape[2:], BF16)

    def rows(m, r):
        return pl.ds(m * blk * d + r, blk, stride=d) if d > 1 else pl.ds(m * blk, blk)

    def prep(x, gain, cos, sin):
        y = _rms(x, gain)
        return (y * cos + pltpu.roll(y, HEAD_DIM // 2, 1) * sin).astype(BF16)

    for m in range(nblk):
        for r in range(d):
            cur = rows(m, r)
            cos_c, sin_c = cos_ref[cur, :], sin_ref[cur, :]
            slot = pl.ds((m * d + r) * blk, blk)
            prev_half = par if m > 0 else 1 - par
            prev_slot = pl.ds(((m - 1 if m > 0 else nblk - 1) * d + r) * blk, blk)
            mask = band if m > 0 else band_first
            for h in range(DIL_HEADS_PER_GROUP):
                k_prev = kcache_ref[prev_half, g, h, prev_slot, :]
                v_prev = vcache_ref[prev_half, g, h, prev_slot, :]
                q = prep(q_refs[h][cur, :], gq, cos_c, sin_c)
                k_cur = prep(k_refs[h][cur, :], gk, cos_c, sin_c)
                v_cur = v_refs[h][cur, :].astype(BF16)
                kcache_ref[par, g, h, slot, :] = k_cur
                vcache_ref[par, g, h, slot, :] = v_cur
                keys = jnp.concatenate([k_prev, k_cur], axis=0)
                vals = jnp.concatenate([jnp.concatenate([v_prev, v_cur], axis=0), ones], axis=1)
                s = lax.dot_general(q, keys, nt, preferred_element_type=F32) * scale
                s = jnp.where(mask, s, NEG_INF)
                mx = jnp.max(s, axis=-1, keepdims=True)
                acc = jnp.dot(jnp.exp(s - mx).astype(BF16), vals, preferred_element_type=F32)
                den = acc[:, HEAD_DIM:]
                oacc_ref[g, h, cur, :] = acc[:, :HEAD_DIM] / den
                lse_ref[g, h, cur, :] = mx + jnp.log(den)


def _dil_kernel(*refs):
    nh = DIL_HEADS_PER_GROUP
    q_refs, k_refs, v_refs = (refs[i * nh:(i + 1) * nh] for i in range(3))
    cos_ref, sin_ref, gq_ref, gk_ref, o_ref, kcache_ref, vcache_ref, oacc_ref, lse_ref = refs[3 * nh:]
    chunk = pl.program_id(1)
    g = pl.program_id(2)
    gq, gk = gq_ref[...], gk_ref[...]
    for gi, (_, d) in enumerate(DIL_PATTERNS):
        @pl.when(g == gi)
        def _(gi=gi, d=d):
            _dil_group(gi, d, chunk, q_refs, k_refs, v_refs, cos_ref, sin_ref, gq, gk, kcache_ref,
                       vcache_ref, oacc_ref, lse_ref)

    @pl.when(g == len(DIL_PATTERNS) - 1)
    def _():
        for h in range(nh):
            a0, a1, a2 = lse_ref[0, h], lse_ref[1, h], lse_ref[2, h]
            mx = jnp.maximum(jnp.maximum(a0, a1), a2)
            e0, e1, e2 = jnp.exp(a0 - mx), jnp.exp(a1 - mx), jnp.exp(a2 - mx)
            den = e0 + e1 + e2
            o_ref[:, h * HEAD_DIM:(h + 1) * HEAD_DIM] = (
                (e0 / den) * oacc_ref[0, h] + (e1 / den) * oacc_ref[1, h] + (e2 / den) * oacc_ref[2, h]
            ).astype(o_ref.dtype)


def dil_attention(p, gq, gk, cos, sin, B, S):
    T = B * S
    nch = S // DIL_CHUNK
    ng, nh = len(DIL_PATTERNS), DIL_HEADS_PER_GROUP

    def heads(part):
        return [pl.BlockSpec((DIL_CHUNK, HEAD_DIM),
                             lambda b, n, g, h=h: (b * nch + n, part * DIL_HEADS + g * nh + h)) for h in range(nh)]

    tab = pl.BlockSpec((DIL_CHUNK, HEAD_DIM), lambda b, n, g: (b * nch + n, 0))
    gain = pl.BlockSpec((1, HEAD_DIM), lambda b, n, g: (0, 0))
    return pl.pallas_call(
        _dil_kernel,
        grid=(B, nch, ng),
        in_specs=heads(0) + heads(1) + heads(2) + [tab, tab, gain, gain],
        out_specs=pl.BlockSpec((DIL_CHUNK, DIL_OUT_WIDTH), lambda b, n, g: (b * nch + n, 0)),
        out_shape=jax.ShapeDtypeStruct((T, DIL_OUT_WIDTH), BF16),
        scratch_shapes=[pltpu.VMEM((2, ng, nh, DIL_CHUNK, HEAD_DIM), BF16),
                        pltpu.VMEM((2, ng, nh, DIL_CHUNK, HEAD_DIM), BF16),
                        pltpu.VMEM((ng, nh, DIL_CHUNK, HEAD_DIM), F32),
                        pltpu.VMEM((ng, nh, DIL_CHUNK, HEAD_DIM), F32)],
        compiler_params=_cparams("arbitrary", "arbitrary", "arbitrary"),
        name="dil_attention",
    )(*([p] * (3 * nh)), cos, sin, gq[None, :], gk[None, :])


def _mla_prep_kernel(lat_ref, gcq_ref, gckv_ref, wqn_ref, wqr_ref, wkv_ref, gq_ref, gk_ref,
                     cos_ref, sin_ref, q_ref, k_ref, v_ref):
    lat = lat_ref[...]
    cq = _rms(lat[:, :MLA_Q_RANK], gcq_ref[...]).astype(BF16)
    ckv = _rms(lat[:, MLA_Q_RANK:MLA_Q_RANK + MLA_KV_RANK], gckv_ref[...]).astype(BF16)
    kpe = lat[:, MLA_Q_RANK + MLA_KV_RANK:]
    qn = jnp.dot(cq, wqn_ref[...], preferred_element_type=F32)
    qr = jnp.dot(cq, wqr_ref[...], preferred_element_type=F32)
    kv = jnp.dot(ckv, wkv_ref[...], preferred_element_type=F32)
    cos, sin = cos_ref[...], sin_ref[...]
    gq, gk = gq_ref[...], gk_ref[...]
    kpe_ss = jnp.sum(kpe * kpe, axis=-1, keepdims=True)

    def rope64(y):
        return y * cos + (pltpu.roll(y, 32, 1) + pltpu.roll(y, 96, 1)) * sin

    for h in range(MLA_HEADS):
        a = slice(h * 128, (h + 1) * 128)
        n0 = h * MLA_HEAD_PAD
        qn_h, qr_h = qn[:, a], qr[:, a]
        ms = (jnp.sum(qn_h * qn_h, axis=-1, keepdims=True)
              + jnp.sum(qr_h * qr_h, axis=-1, keepdims=True)) * (1.0 / MLA_QK)
        r = lax.rsqrt(ms + EPS)
        q_ref[:, n0:n0 + 128] = (qn_h * r * gq[:, :128]).astype(BF16)
        q_ref[:, n0 + 128:n0 + 256] = rope64(qr_h * r * gq[:, 128:]).astype(BF16)
        kn_h = kv[:, n0:n0 + 128]
        ms = (jnp.sum(kn_h * kn_h, axis=-1, keepdims=True) + kpe_ss) * (1.0 / MLA_QK)
        r = lax.rsqrt(ms + EPS)
        k_ref[:, n0:n0 + 128] = (kn_h * r * gk[:, :128]).astype(BF16)
        k_ref[:, n0 + 128:n0 + 256] = rope64(kpe * r * gk[:, 128:]).astype(BF16)
        v_ref[:, n0:n0 + 128] = kv[:, n0 + 128:n0 + 256].astype(BF16)
        v_ref[:, n0 + 128:n0 + 256] = jnp.ones((kv.shape[0], 128), BF16)


def mla_prep(lat, gcq, gckv, wqn, wqr, wkv, gq, gk, cos, sin, tm=512):
    T = lat.shape[0]
    full = lambda a: pl.BlockSpec(a.shape, lambda i: (0,) * a.ndim)
    rows = lambda w: pl.BlockSpec((tm, w), lambda i: (i, 0))
    return pl.pallas_call(
        _mla_prep_kernel,
        grid=(T // tm,),
        in_specs=[rows(lat.shape[1]), full(gcq), full(gckv), full(wqn), full(wqr), full(wkv),
                  full(gq), full(gk), rows(128), rows(128)],
        out_specs=[rows(MLA_HEADS * MLA_HEAD_PAD)] * 3,
        out_shape=[jax.ShapeDtypeStruct((T, MLA_HEADS * MLA_HEAD_PAD), BF16)] * 3,
        compiler_params=_cparams("parallel"),
        name="mla_prep",
    )(lat, gcq, gckv, wqn, wqr, wkv, gq, gk, cos, sin)


def _flash_kernel(q_ref, k_ref, v_ref, o_ref, acc_ref, m_ref, *, t):
    qi = pl.program_id(2)
    q = q_ref[...]
    nt = (((1,), (1,)), ((), ()))
    chunks = [slice(c * 128, (c + 1) * 128) for c in range(t // 128)]

    def probs(s, m):
        return jnp.concatenate([jnp.exp2(s[:, c] - m) for c in chunks], axis=1).astype(BF16)

    kd = pl.multiple_of(qi * t, t)
    s = lax.dot_general(q, k_ref[pl.ds(kd, t), :], nt, preferred_element_type=F32)
    r = lax.broadcasted_iota(jnp.int32, (t, t), 0)
    c = lax.broadcasted_iota(jnp.int32, (t, t), 1)
    s = jnp.where(c <= r, s, NEG_INF)
    m0 = jnp.broadcast_to(jnp.max(s, axis=-1, keepdims=True), (t, 128))
    m_ref[...] = m0
    acc_ref[...] = jnp.dot(probs(s, m0), v_ref[pl.ds(kd, t), :], preferred_element_type=F32)

    def rescaled(ks):
        s2 = lax.dot_general(q_ref[...], k_ref[pl.ds(ks, t), :], nt, preferred_element_type=F32)
        m_old = m_ref[...]
        m_new = jnp.maximum(m_old, jnp.max(s2, axis=-1, keepdims=True))
        alpha = jnp.exp2(m_old - m_new)
        pv2 = jnp.dot(probs(s2, m_new), v_ref[pl.ds(ks, t), :], preferred_element_type=F32)
        acc_ref[...] = jnp.concatenate([alpha, alpha], axis=1) * acc_ref[...] + pv2
        m_ref[...] = m_new

    def blocks(starts):
        m = m_ref[...]
        over, pv = None, None
        for ks in starts:
            s = lax.dot_general(q, k_ref[pl.ds(ks, t), :], nt, preferred_element_type=F32)
            d = [s[:, c] - m for c in chunks]
            over = functools.reduce(jnp.maximum, d if over is None else d + [over])
            p = jnp.concatenate([jnp.exp2(dc) for dc in d], axis=1).astype(BF16)
            part = jnp.dot(p, v_ref[pl.ds(ks, t), :], preferred_element_type=F32)
            pv = part if pv is None else pv + part
        ok = jnp.max(over) <= FLASH_SLACK

        @pl.when(ok)
        def _():
            acc_ref[...] += pv

        @pl.when(jnp.logical_not(ok))
        def _():
            for ks in starts:
                rescaled(ks)

    def group(kg, carry):
        ks = pl.multiple_of(kg * (FLASH_GROUP * t), FLASH_GROUP * t)
        blocks([pl.multiple_of(ks + i * t, t) for i in range(FLASH_GROUP)])
        return carry

    def single(kb, carry):
        blocks([pl.multiple_of(kb * t, t)])
        return carry

    n_grouped = (qi // FLASH_GROUP) * FLASH_GROUP
    lax.fori_loop(0, qi // FLASH_GROUP, group, 0)
    lax.fori_loop(n_grouped, qi, single, 0)

    o_ref[...] = (acc_ref[:, :MLA_V] / acc_ref[:, MLA_V:]).astype(o_ref.dtype)


def mla_flash(q, k, v1, B, S, t=512):
    T = B * S
    nq = S // t
    kern = functools.partial(_flash_kernel, t=t)
    return pl.pallas_call(
        kern,
        grid=(B, MLA_HEADS, nq),
        in_specs=[pl.BlockSpec((t, MLA_HEAD_PAD), lambda b, h, i: (b * nq + i, h)),
                  pl.BlockSpec((S, MLA_HEAD_PAD), lambda b, h, i: (b, h)),
                  pl.BlockSpec((S, 2 * MLA_V), lambda b, h, i: (b, h))],
        out_specs=pl.BlockSpec((t, MLA_V), lambda b, h, i: (b * nq + i, h)),
        out_shape=jax.ShapeDtypeStruct((T, MLA_HEADS * MLA_V), BF16),
        scratch_shapes=[pltpu.VMEM((t, 2 * MLA_V), F32), pltpu.VMEM((t, 128), F32)],
        compiler_params=_cparams("parallel", "parallel", "arbitrary"),
        name="mla_flash",
    )(q, k, v1)


def _merge_kernel(x_ref, g_ref, osb_ref, odil_ref, omla_ref, wg0_ref, wg1_ref, wg2_ref,
                  wsb_ref, wdil_ref, wmla_ref, wout_ref, o_ref, u_ref):
    j = pl.program_id(1)

    @pl.when(j == 0)
    def _():
        x = x_ref[...]
        u_ref[...] = _rms(x, g_ref[...]).astype(BF16)
        o_ref[...] = x

    u = u_ref[...]

    def branch(o_b, wg, wb):
        gate = jax.nn.sigmoid(jnp.dot(u, wg[...], preferred_element_type=F32))
        return gate * jnp.dot(o_b[...], wb[...], preferred_element_type=F32)

    merged = (branch(osb_ref, wg0_ref, wsb_ref) + branch(odil_ref, wg1_ref, wdil_ref)
              + branch(omla_ref, wg2_ref, wmla_ref))
    o_ref[...] += jnp.dot(merged.astype(BF16), wout_ref[...], preferred_element_type=F32)


def merge_out(x, g, osb, odil, omla, wg, wsb, wdil, wmla, wout, tm=512, tn=256):
    T, D = x.shape
    nj = D // tn
    rows = lambda w: pl.BlockSpec((tm, w), lambda i, j: (i, 0))
    cols = lambda k: pl.BlockSpec((k, tn), lambda i, j: (0, j))
    gcol = lambda b: pl.BlockSpec((D, tn), lambda i, j: (0, b * nj + j))
    return pl.pallas_call(
        _merge_kernel,
        grid=(T // tm, nj),
        in_specs=[rows(D), pl.BlockSpec((1, D), lambda i, j: (0, 0)),
                  rows(osb.shape[1]), rows(odil.shape[1]), rows(omla.shape[1]),
                  gcol(0), gcol(1), gcol(2),
                  cols(wsb.shape[0]), cols(wdil.shape[0]), cols(wmla.shape[0]),
                  pl.BlockSpec((tn, D), lambda i, j: (j, 0))],
        out_specs=rows(D),
        out_shape=jax.ShapeDtypeStruct((T, D), F32),
        scratch_shapes=[pltpu.VMEM((tm, D), BF16)],
        compiler_params=_cparams("parallel", "arbitrary"),
        name="merge_out",
    )(x, g[None, :], osb, odil, omla, wg, wg, wg, wsb, wdil, wmla, wout)


def _top_desc(s, k, with_rank=False):
    vals = []
    rank = jnp.full_like(s, float(k))
    for r in range(k):
        mx = jnp.max(s, axis=0, keepdims=True)
        vals.append(mx)
        hit = s == mx
        if with_rank:
            rank = jnp.where(hit, float(r), rank)
        s = jnp.where(hit, -jnp.inf, s)
    vals = jnp.concatenate(vals, axis=0)
    return (vals, rank) if with_rank else vals


def _peer_front_kernel(h_ref, g_ref, wq_ref, keys_ref, xT_ref, r2_ref, e2_ref, n1_ref, e1_ref):
    hn = _rms(h_ref[...], g_ref[...])
    xT_ref[...] = hn.T.astype(BF16)
    q = jnp.dot(hn.astype(BF16), wq_ref[...], preferred_element_type=F32).astype(BF16)
    nt = (((1,), (1,)), ((), ()))
    for h in range(PEER_HEADS):
        c0 = h * 2 * PEER_HALF
        s1 = lax.dot_general(keys_ref[0], q[:, c0:c0 + PEER_HALF], nt, preferred_element_type=F32)
        s2 = lax.dot_general(keys_ref[1], q[:, c0 + PEER_HALF:c0 + 2 * PEER_HALF], nt,
                             preferred_element_type=F32)
        t1 = _top_desc(s1, PEER_TOPK)
        t2, rank2 = _top_desc(s2, PEER_TOPK, with_rank=True)
        cand = jnp.concatenate([t1[0:1] + t2] + [t1[a:a + 1] + t2[0:8] for a in range(1, 8)]
                               + [t1[8:16] + t2[0:1]], axis=0)
        best = _top_desc(cand, PEER_TOPK)
        tau = best[PEER_TOPK - 1:PEER_TOPK]
        z = jnp.sum(jnp.where(cand >= tau, jnp.exp(cand - best[0:1]), 0.0), axis=0, keepdims=True)
        n_sorted = jnp.zeros_like(t1)
        for b in range(PEER_TOPK):
            n_sorted = n_sorted + jnp.where(t1 + t2[b:b + 1] >= tau, 1.0, 0.0)
        n1 = jnp.zeros_like(s1)
        for a in range(PEER_TOPK):
            n1 = jnp.where(s1 == t1[a:a + 1], n_sorted[a:a + 1], n1)
        r2_ref[h] = rank2.astype(BF16)
        e2_ref[h] = jnp.exp(s2 - t2[0:1]).astype(BF16)
        n1_ref[h] = n1
        e1_ref[h] = jnp.exp(s1 - t1[0:1]) / z


def peer_front(h, g, wq, keys, tm=256):
    T, D = h.shape
    tok = lambda: pl.BlockSpec((PEER_HEADS, PEER_NKEYS, tm), lambda i: (0, 0, i))
    tok_shape = jax.ShapeDtypeStruct((PEER_HEADS, PEER_NKEYS, T), F32)
    tok_half = jax.ShapeDtypeStruct((PEER_HEADS, PEER_NKEYS, T), BF16)
    return pl.pallas_call(
        _peer_front_kernel,
        grid=(T // tm,),
        in_specs=[pl.BlockSpec((tm, D), lambda i: (i, 0)),
                  pl.BlockSpec((1, D), lambda i: (0, 0)),
                  pl.BlockSpec(wq.shape, lambda i: (0, 0)),
                  pl.BlockSpec(keys.shape, lambda i: (0, 0, 0))],
        out_specs=[pl.BlockSpec((D, tm), lambda i: (0, i)), tok(), tok(), tok(), tok()],
        out_shape=[jax.ShapeDtypeStruct((D, T), BF16), tok_half, tok_half, tok_shape, tok_shape],
        compiler_params=_cparams("parallel"),
        name="peer_front",
    )(h, g[None, :], wq, keys)


def _transpose_cast_kernel(x_ref, o_ref):
    o_ref[...] = x_ref[...].T.astype(o_ref.dtype)


def transpose_cast(x, dtype, tr=512):
    R, C = x.shape
    return pl.pallas_call(
        _transpose_cast_kernel,
        grid=(R // tr,),
        in_specs=[pl.BlockSpec((tr, C), lambda i: (i, 0))],
        out_specs=pl.BlockSpec((C, tr), lambda i: (0, i)),
        out_shape=jax.ShapeDtypeStruct((C, R), dtype),
        compiler_params=_cparams("parallel"),
        name="transpose_cast",
    )(x)


PEER_CHUNKS = 4
PEER_HALF_ROWS = PEER_CHUNKS * PEER_NKEYS


def _peer_dense_kernel(xT_ref, u0_ref, ub_ref, ua_ref, vT_ref, r2_ref, e2_ref, n1_ref, e1_ref, h_ref, o_ref,
                       acc_ref, aa_ref, ab_ref):
    e = pl.program_id(1)

    @pl.when(e == 0)
    def _():
        acc_ref[...] = jnp.zeros_like(acc_ref)
        aa_ref[...] = jnp.dot(u0_ref[...], xT_ref[...], preferred_element_type=F32)

    def hidden(a_ref, k0):
        parts = []
        for c in range(PEER_CHUNKS):
            k = k0 + c
            gate = None
            for h in range(PEER_HEADS):
                n1 = n1_ref[h, k:k + 1, :].astype(BF16)
                e1 = e1_ref[h, k:k + 1, :].astype(BF16)
                w = jnp.where(r2_ref[h] < n1, e2_ref[h] * e1, jnp.zeros((), BF16))
                gate = w if gate is None else gate + w
            a = a_ref[c * PEER_NKEYS:(c + 1) * PEER_NKEYS, :]
            gelu = 0.5 * a * (1.0 + lax.erf(a * (2.0 ** -0.5)))
            parts.append(gelu.astype(BF16) * gate)
        return jnp.concatenate(parts, axis=0)

    xT = xT_ref[...]
    ab_ref[...] = jnp.dot(ub_ref[...], xT, preferred_element_type=F32)
    h_a = hidden(aa_ref, 0)
    acc_ref[...] += jnp.dot(vT_ref[:, :PEER_HALF_ROWS], h_a, preferred_element_type=F32)
    aa_ref[...] = jnp.dot(ua_ref[...], xT, preferred_element_type=F32)
    h_b = hidden(ab_ref, PEER_CHUNKS)
    acc_ref[...] += jnp.dot(vT_ref[:, PEER_HALF_ROWS:], h_b, preferred_element_type=F32)

    @pl.when(e == pl.num_programs(1) - 1)
    def _():
        o_ref[...] = h_ref[...] + acc_ref[...].T


def peer_dense(xT, u, vT, r2, e2, n1, e1, h, tm=512):
    D, T = xT.shape
    E = u.shape[0]
    hr = PEER_HALF_ROWS
    ne = E // (2 * hr)
    ni = 2 * PEER_CHUNKS
    tok = lambda: pl.BlockSpec((PEER_HEADS, PEER_NKEYS, tm), lambda i, e: (0, 0, i))
    key = lambda: pl.BlockSpec((PEER_HEADS, ni, tm), lambda i, e: (0, e, i))
    return pl.pallas_call(
        _peer_dense_kernel,
        grid=(T // tm, ne),
        in_specs=[pl.BlockSpec((D, tm), lambda i, e: (0, i)),
                  pl.BlockSpec((hr, D), lambda i, e: (0, 0)),
                  pl.BlockSpec((hr, D), lambda i, e: (2 * e + 1, 0)),
                  pl.BlockSpec((hr, D), lambda i, e: (jnp.minimum(2 * e + 2, 2 * ne - 1), 0)),
                  pl.BlockSpec((D, 2 * hr), lambda i, e: (0, e)),
                  tok(), tok(), key(), key(),
                  pl.BlockSpec((tm, D), lambda i, e: (i, 0))],
        out_specs=pl.BlockSpec((tm, D), lambda i, e: (i, 0)),
        out_shape=jax.ShapeDtypeStruct((T, D), F32),
        scratch_shapes=[pltpu.VMEM((D, tm), F32), pltpu.VMEM((hr, tm), F32), pltpu.VMEM((hr, tm), F32)],
        compiler_params=_cparams("parallel", "arbitrary"),
        name="peer_dense",
    )(xT, u, u, u, vT, r2, e2, n1, e1, h)


def _pad_cols(w, n):
    return jnp.pad(w, ((0, 0), (0, n - w.shape[1])))


def _layer(x, B, S, tabs, norm_mix, w_in, dil_q_norm, dil_k_norm, mla_cq_norm, mla_ckv_norm, mla_w_uq,
           mla_w_ukv, mla_q_norm, mla_k_norm, w_branch_sb, w_branch_dil, w_branch_mla, w_out, norm_ffn,
           peer_w_query, peer_sub_keys, peer_u, peer_v):
    cos128, sin128, cos64, sin64 = tabs
    c_dil, c_lat = 3 * SB_WIDTH, 3 * SB_WIDTH + 3 * DIL_WIDTH

    w_c = _pad_cols(w_in[:, c_lat:MIX_COLS], 896)
    w_g = w_in[:, MIX_COLS:]
    p_a = norm_matmul(x, norm_mix, w_in, BF16, tn=768, col0=0, n_cols=c_dil, name="in_proj_a")
    p_b = norm_matmul(x, norm_mix, w_in, F32, tn=768, col0=c_dil, n_cols=c_lat - c_dil, name="in_proj_b")
    p_c = norm_matmul(x, norm_mix, w_c, F32, tn=896, name="in_proj_c")

    o_sb = sb_attention(p_a, B, S)

    o_dil = dil_attention(p_b, dil_q_norm, dil_k_norm, cos128, sin128, B, S)

    uq = mla_w_uq.reshape(MLA_Q_RANK, MLA_HEADS, MLA_QK)
    wqn = uq[:, :, :MLA_NOPE].reshape(MLA_Q_RANK, MLA_HEADS * MLA_NOPE).astype(BF16)
    wqr = jnp.pad(uq[:, :, MLA_NOPE:], ((0, 0), (0, 0), (0, 128 - MLA_ROPE)))
    wqr = wqr.reshape(MLA_Q_RANK, MLA_HEADS * 128).astype(BF16)
    gpad = lambda gn: jnp.pad(gn, (0, MLA_HEAD_PAD - MLA_QK))[None, :]
    q, k, v = mla_prep(p_c, mla_cq_norm[None, :], mla_ckv_norm[None, :], wqn, wqr, mla_w_ukv.astype(BF16),
                       gpad(mla_q_norm * (MLA_QK ** -0.5 * LOG2E)), gpad(mla_k_norm), cos64, sin64)
    o_mla = mla_flash(q, k, v, B, S)

    h = merge_out(x, norm_mix, o_sb, o_dil, o_mla, w_g, w_branch_sb.astype(BF16), w_branch_dil.astype(BF16),
                  w_branch_mla.astype(BF16), w_out.astype(BF16))

    xT, r2, e2, n1, e1 = peer_front(h, norm_ffn, peer_w_query.astype(BF16), peer_sub_keys.astype(BF16))
    return peer_dense(xT, peer_u.astype(BF16), transpose_cast(peer_v, BF16), r2, e2, n1, e1, h)


def kernel(x, positions, norm_mix, w_in, dil_q_norm, dil_k_norm, mla_cq_norm, mla_ckv_norm, mla_w_uq, mla_w_ukv,
           mla_q_norm, mla_k_norm, w_branch_sb, w_branch_dil, w_branch_mla, w_out, norm_ffn, peer_w_query,
           peer_sub_keys, peer_u, peer_v):
    B, S, D = x.shape
    pos_col = positions.reshape(B * S, 1)
    tabs = rope_tables(pos_col, HEAD_DIM // 2) + rope_tables(pos_col, MLA_ROPE // 2)
    xf = x.reshape(B * S, D)
    w_in_bf = w_in.astype(BF16)
    for l in range(w_in.shape[0]):
        xf = _layer(xf, B, S, tabs, norm_mix[l], w_in_bf[l], dil_q_norm[l], dil_k_norm[l], mla_cq_norm[l],
                    mla_ckv_norm[l], mla_w_uq[l], mla_w_ukv[l], mla_q_norm[l], mla_k_norm[l], w_branch_sb[l],
                    w_branch_dil[l], w_branch_mla[l], w_out[l], norm_ffn[l], peer_w_query[l],
                    peer_sub_keys[l], peer_u[l], peer_v[l])
    return xf.reshape(B, S, D)
```

```python
import functools

import numpy as np
import jax
import jax.numpy as jnp
from jax import lax
from jax.experimental import pallas as pl
from jax.experimental.pallas import tpu as pltpu

F32 = jnp.float32
BF16 = jnp.bfloat16

D_MODEL = 2048
HEAD_DIM = 128
ROPE_THETA = 10000.0
EPS = 1e-6
NEG_INF = -1e30
LOG2E = 1.4426950408889634

SB_HEADS = 4
SB_WIDTH = SB_HEADS * HEAD_DIM
DIL_PATTERNS = ((128, 1), (512, 4), (2048, 16))
DIL_HEADS_PER_GROUP = 2
DIL_HEADS = DIL_HEADS_PER_GROUP * len(DIL_PATTERNS)
DIL_WIDTH = DIL_HEADS * HEAD_DIM
DIL_OUT_WIDTH = DIL_HEADS_PER_GROUP * HEAD_DIM
DIL_BLOCK = 128
MLA_HEADS = 6
MLA_NOPE = 128
MLA_ROPE = 64
MLA_V = 128
MLA_QK = MLA_NOPE + MLA_ROPE
MLA_Q_RANK = 512
MLA_KV_RANK = 256
MLA_HEAD_PAD = 256
N_BRANCH = 3
MIX_COLS = 3 * SB_WIDTH + 3 * DIL_WIDTH + MLA_Q_RANK + MLA_KV_RANK + MLA_ROPE
PEER_HEADS = 8
PEER_NKEYS = 128
PEER_EXPERTS = PEER_NKEYS * PEER_NKEYS
PEER_HALF = 128
PEER_TOPK = 16

SB_EXIT = -104.0

FLASH_SLACK = 60.0
FLASH_GROUP = 4

VMEM_LIMIT = 56 * 1024 * 1024


def _cparams(*sem):
    return pltpu.CompilerParams(dimension_semantics=sem, vmem_limit_bytes=VMEM_LIMIT)


def _rms(x, g):
    return x * lax.rsqrt(jnp.mean(x * x, axis=-1, keepdims=True) + EPS) * g


def _rope_table_kernel(pos_ref, inv_ref, sign_ref, cos_ref, sin_ref):
    ang = pos_ref[...].astype(F32) * inv_ref[...]
    cos_ref[...] = jnp.cos(ang)
    sin_ref[...] = jnp.sin(ang) * sign_ref[...]


def rope_tables(pos_col, half, tm=2048):
    T = pos_col.shape[0]
    inv_h = ROPE_THETA ** (-jnp.arange(half, dtype=F32) / half)
    pad = jnp.zeros((128 - 2 * half,), F32)
    inv = jnp.concatenate([inv_h, inv_h, pad])[None, :]
    sign = jnp.concatenate([-jnp.ones((half,), F32), jnp.ones((half,), F32), pad])[None, :]
    row = pl.BlockSpec((1, 128), lambda i: (0, 0))
    cos, sin = pl.pallas_call(
        _rope_table_kernel,
        grid=(T // tm,),
        in_specs=[pl.BlockSpec((tm, 1), lambda i: (i, 0)), row, row],
        out_specs=[pl.BlockSpec((tm, 128), lambda i: (i, 0))] * 2,
        out_shape=[jax.ShapeDtypeStruct((T, 128), F32)] * 2,
        compiler_params=_cparams("parallel"),
        name="rope_tables",
    )(pos_col, inv, sign)
    return cos, sin


def _norm_matmul_kernel(x_ref, g_ref, w_ref, o_ref, u_ref):
    @pl.when(pl.program_id(1) == 0)
    def _():
        u_ref[...] = _rms(x_ref[...], g_ref[...]).astype(BF16)

    o_ref[...] = jnp.dot(u_ref[...], w_ref[...], preferred_element_type=F32).astype(o_ref.dtype)


def norm_matmul(x, g, w, out_dtype, tn, tm=512, name="norm_matmul"):
    T, D = x.shape
    N = w.shape[1]
    return pl.pallas_call(
        _norm_matmul_kernel,
        grid=(T // tm, N // tn),
        in_specs=[pl.BlockSpec((tm, D), lambda i, j: (i, 0)),
                  pl.BlockSpec((1, D), lambda i, j: (0, 0)),
                  pl.BlockSpec((D, tn), lambda i, j: (0, j))],
        out_specs=pl.BlockSpec((tm, tn), lambda i, j: (i, j)),
        out_shape=jax.ShapeDtypeStruct((T, N), out_dtype),
        scratch_shapes=[pltpu.VMEM((tm, D), BF16)],
        compiler_params=_cparams("parallel", "arbitrary"),
        name=name,
    )(x, g[None, :], w)


def _norm_matmul2_kernel(x_ref, g_ref, w_ref, oa_ref, ob_ref, u_ref, *, na):
    j = pl.program_id(1)

    @pl.when(j == 0)
    def _():
        u_ref[...] = _rms(x_ref[...], g_ref[...]).astype(BF16)

    res = jnp.dot(u_ref[...], w_ref[...], preferred_element_type=F32)

    @pl.when(j < na)
    def _():
        oa_ref[...] = res.astype(oa_ref.dtype)

    @pl.when(j >= na)
    def _():
        ob_ref[...] = res.astype(ob_ref.dtype)


def norm_matmul2(x, g, w3, layer, n_a, n_b, dtype_a, dtype_b, tn, tm=512, name="norm_matmul2"):
    T, D = x.shape
    na, nb = n_a // tn, n_b // tn
    kern = functools.partial(_norm_matmul2_kernel, na=na)
    return pl.pallas_call(
        kern,
        grid=(T // tm, na + nb),
        in_specs=[pl.BlockSpec((tm, D), lambda i, j: (i, 0)),
                  pl.BlockSpec((1, D), lambda i, j: (0, 0)),
                  pl.BlockSpec((None, D, tn), lambda i, j: (layer, 0, j))],
        out_specs=[pl.BlockSpec((tm, tn), lambda i, j: (i, jnp.minimum(j, na - 1))),
                   pl.BlockSpec((tm, tn), lambda i, j: (i, jnp.maximum(j - na, 0)))],
        out_shape=[jax.ShapeDtypeStruct((T, n_a), dtype_a), jax.ShapeDtypeStruct((T, n_b), dtype_b)],
        scratch_shapes=[pltpu.VMEM((tm, D), BF16)],
        compiler_params=_cparams("parallel", "arbitrary"),
        name=name,
    )(x, g[None, :], w3)


def _sb_kernel(q_ref, k_ref, v_ref, tri_ref, o_ref, acc_ref, carry_ref, *, tq, tk, scale):
    q0 = pl.program_id(1) * tq
    acc_ref[...] = jnp.zeros_like(acc_ref)
    carry_ref[...] = jnp.zeros_like(carry_ref)
    tri = tri_ref[...]
    row = q0 + lax.broadcasted_iota(jnp.int32, (tq, tk), 0)
    lane = lax.broadcasted_iota(jnp.int32, (tq, tk), 1)

    def cond(c):
        kb, top = c
        return jnp.logical_and(kb >= 0, top > SB_EXIT)

    def body(c):
        kb, _ = c
        ks = pl.multiple_of(kb * tk, tk)
        mask = (ks + lane) < row
        tops = []
        for h in range(SB_HEADS):
            sl = slice(h * HEAD_DIM, (h + 1) * HEAD_DIM)
            k = k_ref[pl.ds(ks, tk), sl]
            v = v_ref[pl.ds(ks, tk), sl]
            z = lax.dot_general(q_ref[:, sl], k, (((1,), (1,)), ((), ())), preferred_element_type=F32) * scale
            lk = -(jnp.maximum(z, 0.0) + jnp.log1p(jnp.exp(-jnp.abs(z))))
            lk = jnp.where(mask, lk, 0.0)
            hi = lk.astype(BF16)
            lo = (lk - hi.astype(F32)).astype(BF16)
            loc = (jnp.dot(hi, tri, preferred_element_type=F32)
                   + jnp.dot(lo, tri, preferred_element_type=F32))
            carry = carry_ref[:, sl]
            after = carry + loc[:, :tk]
            w = jnp.where(mask, jnp.exp(z + lk + after), 0.0)
            acc_ref[:, sl] += jnp.dot(w.astype(BF16), v, preferred_element_type=F32)
            new_carry = carry + loc[:, tk:]
            carry_ref[:, sl] = new_carry
            tops.append(jnp.max(new_carry))
        return kb - 1, functools.reduce(jnp.maximum, tops)

    lax.while_loop(cond, body, ((q0 + tq) // tk - 1, jnp.float32(0.0)))
    o_ref[...] = acc_ref[...].astype(o_ref.dtype)


def sb_attention(qkv, B, S, tq=256, tk=HEAD_DIM):
    T = B * S
    nq = S // tq
    tri = np.concatenate([np.triu(np.ones((tk, tk), np.float32), 1).T, np.ones((tk, tk), np.float32)], axis=1)
    tri = jnp.asarray(tri, BF16)
    kern = functools.partial(_sb_kernel, tq=tq, tk=tk, scale=HEAD_DIM ** -0.5)
    return pl.pallas_call(
        kern,
        grid=(B, nq),
        in_specs=[pl.BlockSpec((tq, SB_WIDTH), lambda b, i: (b * nq + i, 0)),
                  pl.BlockSpec((S, SB_WIDTH), lambda b, i: (b, 1)),
                  pl.BlockSpec((S, SB_WIDTH), lambda b, i: (b, 2)),
                  pl.BlockSpec((tk, 2 * tk), lambda b, i: (0, 0))],
        out_specs=pl.BlockSpec((tq, SB_WIDTH), lambda b, i: (b * nq + i, 0)),
        out_shape=jax.ShapeDtypeStruct((T, SB_WIDTH), BF16),
        scratch_shapes=[pltpu.VMEM((tq, SB_WIDTH), F32), pltpu.VMEM((tq, SB_WIDTH), F32)],
        compiler_params=_cparams("parallel", "arbitrary"),
        name="sb_attention",
    )(qkv, qkv, qkv, tri)


DIL_CHUNK = 2048


def _dil_group(g, d, chunk, q_refs, k_refs, v_refs, cos_ref, sin_ref, gq, gk, kcache_ref, vcache_ref,
               oacc_ref, lse_ref):
    first_chunk = chunk == 0
    par = chunk % 2
    blk = DIL_BLOCK
    nblk = DIL_CHUNK // (blk * d)
    n_back = DIL_PATTERNS[g][0] // d
    scale = HEAD_DIM ** -0.5
    qi = lax.broadcasted_iota(jnp.int32, (blk, 2 * blk), 0)
    kj = lax.broadcasted_iota(jnp.int32, (blk, 2 * blk), 1)
    dist = blk + qi - kj
    band = jnp.logical_and(dist >= 0, dist <= n_back)
    band_first = jnp.logical_and(band, jnp.logical_or(kj >= blk, jnp.logical_not(first_chunk)))
    nt = (((1,), (1,)), ((), ()))
    ones = jnp.ones((2 * blk, HEAD_DIM), BF16)

    @pl.when(first_chunk)
    def _():
        kcache_ref[1, g] = jnp.zeros(kcache_ref.shape[2:], BF16)
        vcache_ref[1, g] = jnp.zeros(vcache_ref.shape[2:], BF16)

    def rows(m, r):
        return pl.ds(m * blk * d + r, blk, stride=d) if d > 1 else pl.ds(m * blk, blk)

    def prep(x, gain, cos, sin):
        y = _rms(x, gain)
        return (y * cos + pltpu.roll(y, HEAD_DIM // 2, 1) * sin).astype(BF16)

    for m in range(nblk):
        for r in range(d):
            cur = rows(m, r)
            cos_c, sin_c = cos_ref[cur, :], sin_ref[cur, :]
            slot = pl.ds((m * d + r) * blk, blk)
            prev_half = par if m > 0 else 1 - par
            prev_slot = pl.ds(((m - 1 if m > 0 else nblk - 1) * d + r) * blk, blk)
            mask = band if m > 0 else band_first
            for h in range(DIL_HEADS_PER_GROUP):
                k_prev = kcache_ref[prev_half, g, h, prev_slot, :]
                v_prev = vcache_ref[prev_half, g, h, prev_slot, :]
                q = prep(q_refs[h][cur, :], gq, cos_c, sin_c)
                k_cur = prep(k_refs[h][cur, :], gk, cos_c, sin_c)
                v_cur = v_refs[h][cur, :].astype(BF16)
                kcache_ref[par, g, h, slot, :] = k_cur
                vcache_ref[par, g, h, slot, :] = v_cur
                keys = jnp.concatenate([k_prev, k_cur], axis=0)
                vals = jnp.concatenate([jnp.concatenate([v_prev, v_cur], axis=0), ones], axis=1)
                s = lax.dot_general(q, keys, nt, preferred_element_type=F32) * scale
                s = jnp.where(mask, s, NEG_INF)
                mx = jnp.max(s, axis=-1, keepdims=True)
                acc = jnp.dot(jnp.exp(s - mx).astype(BF16), vals, preferred_element_type=F32)
                den = acc[:, HEAD_DIM:]
                oacc_ref[g, h, cur, :] = acc[:, :HEAD_DIM] / den
                lse_ref[g, h, cur, :] = mx + jnp.log(den)


def _dil_kernel(*refs):
    nh = DIL_HEADS_PER_GROUP
    q_refs, k_refs, v_refs = (refs[i * nh:(i + 1) * nh] for i in range(3))
    cos_ref, sin_ref, gq_ref, gk_ref, o_ref, kcache_ref, vcache_ref, oacc_ref, lse_ref = refs[3 * nh:]
    chunk = pl.program_id(1)
    g = pl.program_id(2)
    gq, gk = gq_ref[...], gk_ref[...]
    for gi, (_, d) in enumerate(DIL_PATTERNS):
        @pl.when(g == gi)
        def _(gi=gi, d=d):
            _dil_group(gi, d, chunk, q_refs, k_refs, v_refs, cos_ref, sin_ref, gq, gk, kcache_ref,
                       vcache_ref, oacc_ref, lse_ref)

    @pl.when(g == len(DIL_PATTERNS) - 1)
    def _():
        for h in range(nh):
            a0, a1, a2 = lse_ref[0, h], lse_ref[1, h], lse_ref[2, h]
            mx = jnp.maximum(jnp.maximum(a0, a1), a2)
            e0, e1, e2 = jnp.exp(a0 - mx), jnp.exp(a1 - mx), jnp.exp(a2 - mx)
            den = e0 + e1 + e2
            o_ref[:, h * HEAD_DIM:(h + 1) * HEAD_DIM] = (
                (e0 / den) * oacc_ref[0, h] + (e1 / den) * oacc_ref[1, h] + (e2 / den) * oacc_ref[2, h]
            ).astype(o_ref.dtype)


def dil_attention(p, gq, gk, cos, sin, B, S):
    T = B * S
    nch = S // DIL_CHUNK
    ng, nh = len(DIL_PATTERNS), DIL_HEADS_PER_GROUP

    def heads(part):
        return [pl.BlockSpec((DIL_CHUNK, HEAD_DIM),
                             lambda b, n, g, h=h: (b * nch + n, part * DIL_HEADS + g * nh + h)) for h in range(nh)]

    tab = pl.BlockSpec((DIL_CHUNK, HEAD_DIM), lambda b, n, g: (b * nch + n, 0))
    gain = pl.BlockSpec((1, HEAD_DIM), lambda b, n, g: (0, 0))
    return pl.pallas_call(
        _dil_kernel,
        grid=(B, nch, ng),
        in_specs=heads(0) + heads(1) + heads(2) + [tab, tab, gain, gain],
        out_specs=pl.BlockSpec((DIL_CHUNK, DIL_OUT_WIDTH), lambda b, n, g: (b * nch + n, 0)),
        out_shape=jax.ShapeDtypeStruct((T, DIL_OUT_WIDTH), BF16),
        scratch_shapes=[pltpu.VMEM((2, ng, nh, DIL_CHUNK, HEAD_DIM), BF16),
                        pltpu.VMEM((2, ng, nh, DIL_CHUNK, HEAD_DIM), BF16),
                        pltpu.VMEM((ng, nh, DIL_CHUNK, HEAD_DIM), F32),
                        pltpu.VMEM((ng, nh, DIL_CHUNK, HEAD_DIM), F32)],
        compiler_params=_cparams("arbitrary", "arbitrary", "arbitrary"),
        name="dil_attention",
    )(*([p] * (3 * nh)), cos, sin, gq[None, :], gk[None, :])


def _mla_prep_kernel(lat_ref, gcq_ref, gckv_ref, wqn_ref, wqr_ref, wkv_ref, gq_ref, gk_ref,
                     cos_ref, sin_ref, q_ref, k_ref, v_ref):
    lat = lat_ref[...]
    cq = _rms(lat[:, :MLA_Q_RANK], gcq_ref[...]).astype(BF16)
    ckv = _rms(lat[:, MLA_Q_RANK:MLA_Q_RANK + MLA_KV_RANK], gckv_ref[...]).astype(BF16)
    kpe = lat[:, MLA_Q_RANK + MLA_KV_RANK:]
    qn = jnp.dot(cq, wqn_ref[...], preferred_element_type=F32)
    qr = jnp.dot(cq, wqr_ref[...], preferred_element_type=F32)
    kv = jnp.dot(ckv, wkv_ref[...], preferred_element_type=F32)
    cos, sin = cos_ref[...], sin_ref[...]
    gq, gk = gq_ref[...], gk_ref[...]
    kpe_ss = jnp.sum(kpe * kpe, axis=-1, keepdims=True)

    def rope64(y):
        return y * cos + (pltpu.roll(y, 32, 1) + pltpu.roll(y, 96, 1)) * sin

    for h in range(MLA_HEADS):
        a = slice(h * 128, (h + 1) * 128)
        n0 = h * MLA_HEAD_PAD
        qn_h, qr_h = qn[:, a], qr[:, a]
        ms = (jnp.sum(qn_h * qn_h, axis=-1, keepdims=True)
              + jnp.sum(qr_h * qr_h, axis=-1, keepdims=True)) * (1.0 / MLA_QK)
        r = lax.rsqrt(ms + EPS)
        q_ref[:, n0:n0 + 128] = (qn_h * r * gq[:, :128]).astype(BF16)
        q_ref[:, n0 + 128:n0 + 256] = rope64(qr_h * r * gq[:, 128:]).astype(BF16)
        kn_h = kv[:, n0:n0 + 128]
        ms = (jnp.sum(kn_h * kn_h, axis=-1, keepdims=True) + kpe_ss) * (1.0 / MLA_QK)
        r = lax.rsqrt(ms + EPS)
        k_ref[:, n0:n0 + 128] = (kn_h * r * gk[:, :128]).astype(BF16)
        k_ref[:, n0 + 128:n0 + 256] = rope64(kpe * r * gk[:, 128:]).astype(BF16)
        v_ref[:, n0:n0 + 128] = kv[:, n0 + 128:n0 + 256].astype(BF16)
        v_ref[:, n0 + 128:n0 + 256] = jnp.ones((kv.shape[0], 128), BF16)


def mla_prep(lat, gcq, gckv, wqn, wqr, wkv, gq, gk, cos, sin, tm=512):
    T = lat.shape[0]
    full = lambda a: pl.BlockSpec(a.shape, lambda i: (0,) * a.ndim)
    rows = lambda w: pl.BlockSpec((tm, w), lambda i: (i, 0))
    return pl.pallas_call(
        _mla_prep_kernel,
        grid=(T // tm,),
        in_specs=[rows(lat.shape[1]), full(gcq), full(gckv), full(wqn), full(wqr), full(wkv),
                  full(gq), full(gk), rows(128), rows(128)],
        out_specs=[rows(MLA_HEADS * MLA_HEAD_PAD)] * 3,
        out_shape=[jax.ShapeDtypeStruct((T, MLA_HEADS * MLA_HEAD_PAD), BF16)] * 3,
        compiler_params=_cparams("parallel"),
        name="mla_prep",
    )(lat, gcq, gckv, wqn, wqr, wkv, gq, gk, cos, sin)


def _flash_kernel(q_ref, k_ref, v_ref, o_ref, acc_ref, m_ref, *, t):
    qi = pl.program_id(2)
    q = q_ref[...]
    nt = (((1,), (1,)), ((), ()))
    chunks = [slice(c * 128, (c + 1) * 128) for c in range(t // 128)]

    def probs(s, m):
        return jnp.concatenate([jnp.exp2(s[:, c] - m) for c in chunks], axis=1).astype(BF16)

    kd = pl.multiple_of(qi * t, t)
    s = lax.dot_general(q, k_ref[pl.ds(kd, t), :], nt, preferred_element_type=F32)
    r = lax.broadcasted_iota(jnp.int32, (t, t), 0)
    c = lax.broadcasted_iota(jnp.int32, (t, t), 1)
    s = jnp.where(c <= r, s, NEG_INF)
    m0 = jnp.broadcast_to(jnp.max(s, axis=-1, keepdims=True), (t, 128))
    m_ref[...] = m0
    acc_ref[...] = jnp.dot(probs(s, m0), v_ref[pl.ds(kd, t), :], preferred_element_type=F32)

    def rescaled(ks):
        s2 = lax.dot_general(q_ref[...], k_ref[pl.ds(ks, t), :], nt, preferred_element_type=F32)
        m_old = m_ref[...]
        m_new = jnp.maximum(m_old, jnp.max(s2, axis=-1, keepdims=True))
        alpha = jnp.exp2(m_old - m_new)
        pv2 = jnp.dot(probs(s2, m_new), v_ref[pl.ds(ks, t), :], preferred_element_type=F32)
        acc_ref[...] = jnp.concatenate([alpha, alpha], axis=1) * acc_ref[...] + pv2
        m_ref[...] = m_new

    def blocks(starts):
        m = m_ref[...]
        over, pv = None, None
        for ks in starts:
            s = lax.dot_general(q, k_ref[pl.ds(ks, t), :], nt, preferred_element_type=F32)
            d = [s[:, c] - m for c in chunks]
            over = functools.reduce(jnp.maximum, d if over is None else d + [over])
            p = jnp.concatenate([jnp.exp2(dc) for dc in d], axis=1).astype(BF16)
            part = jnp.dot(p, v_ref[pl.ds(ks, t), :], preferred_element_type=F32)
            pv = part if pv is None else pv + part
        ok = jnp.max(over) <= FLASH_SLACK

        @pl.when(ok)
        def _():
            acc_ref[...] += pv

        @pl.when(jnp.logical_not(ok))
        def _():
            for ks in starts:
                rescaled(ks)

    def group(kg, carry):
        ks = pl.multiple_of(kg * (FLASH_GROUP * t), FLASH_GROUP * t)
        blocks([pl.multiple_of(ks + i * t, t) for i in range(FLASH_GROUP)])
        return carry

    def single(kb, carry):
        blocks([pl.multiple_of(kb * t, t)])
        return carry

    n_grouped = (qi // FLASH_GROUP) * FLASH_GROUP
    lax.fori_loop(0, qi // FLASH_GROUP, group, 0)
    lax.fori_loop(n_grouped, qi, single, 0)

    o_ref[...] = (acc_ref[:, :MLA_V] / acc_ref[:, MLA_V:]).astype(o_ref.dtype)


def mla_flash(q, k, v1, B, S, t=512):
    T = B * S
    nq = S // t
    kern = functools.partial(_flash_kernel, t=t)
    return pl.pallas_call(
        kern,
        grid=(B, MLA_HEADS, nq),
        in_specs=[pl.BlockSpec((t, MLA_HEAD_PAD), lambda b, h, i: (b * nq + i, h)),
                  pl.BlockSpec((S, MLA_HEAD_PAD), lambda b, h, i: (b, h)),
                  pl.BlockSpec((S, 2 * MLA_V), lambda b, h, i: (b, h))],
        out_specs=pl.BlockSpec((t, MLA_V), lambda b, h, i: (b * nq + i, h)),
        out_shape=jax.ShapeDtypeStruct((T, MLA_HEADS * MLA_V), BF16),
        scratch_shapes=[pltpu.VMEM((t, 2 * MLA_V), F32), pltpu.VMEM((t, 128), F32)],
        compiler_params=_cparams("parallel", "parallel", "arbitrary"),
        name="mla_flash",
    )(q, k, v1)


def _merge_kernel(x_ref, g_ref, osb_ref, odil_ref, omla_ref, wg0_ref, wg1_ref, wg2_ref,
                  wsb_ref, wdil_ref, wmla_ref, wout_ref, o_ref, u_ref):
    j = pl.program_id(1)

    @pl.when(j == 0)
    def _():
        x = x_ref[...]
        u_ref[...] = _rms(x, g_ref[...]).astype(BF16)
        o_ref[...] = x

    u = u_ref[...]

    def branch(o_b, wg, wb):
        gate = jax.nn.sigmoid(jnp.dot(u, wg[...], preferred_element_type=F32))
        return gate * jnp.dot(o_b[...], wb[...], preferred_element_type=F32)

    merged = (branch(osb_ref, wg0_ref, wsb_ref) + branch(odil_ref, wg1_ref, wdil_ref)
              + branch(omla_ref, wg2_ref, wmla_ref))
    o_ref[...] += jnp.dot(merged.astype(BF16), wout_ref[...], preferred_element_type=F32)


def merge_out(x, g, osb, odil, omla, wg, wsb, wdil, wmla, wout, tm=512, tn=512):
    T, D = x.shape
    nj = D // tn
    rows = lambda w: pl.BlockSpec((tm, w), lambda i, j: (i, 0))
    cols = lambda k: pl.BlockSpec((k, tn), lambda i, j: (0, j))
    gcol = lambda b: pl.BlockSpec((D, tn), lambda i, j: (0, b * nj + j))
    return pl.pallas_call(
        _merge_kernel,
        grid=(T // tm, nj),
        in_specs=[rows(D), pl.BlockSpec((1, D), lambda i, j: (0, 0)),
                  rows(osb.shape[1]), rows(odil.shape[1]), rows(omla.shape[1]),
                  gcol(0), gcol(1), gcol(2),
                  cols(wsb.shape[0]), cols(wdil.shape[0]), cols(wmla.shape[0]),
                  pl.BlockSpec((tn, D), lambda i, j: (j, 0))],
        out_specs=rows(D),
        out_shape=jax.ShapeDtypeStruct((T, D), F32),
        scratch_shapes=[pltpu.VMEM((tm, D), BF16)],
        compiler_params=_cparams("parallel", "arbitrary"),
        name="merge_out",
    )(x, g[None, :], osb, odil, omla, wg, wg, wg, wsb, wdil, wmla, wout)


def _top_desc(s, k, with_rank=False):
    vals = []
    rank = jnp.full_like(s, float(k))
    for r in range(k):
        mx = jnp.max(s, axis=0, keepdims=True)
        vals.append(mx)
        hit = s == mx
        if with_rank:
            rank = jnp.where(hit, float(r), rank)
        s = jnp.where(hit, -jnp.inf, s)
    vals = jnp.concatenate(vals, axis=0)
    return (vals, rank) if with_rank else vals


def _peer_front_kernel(h_ref, g_ref, wq_ref, keys_ref, xT_ref, r2_ref, e2_ref, n1_ref, e1_ref):
    hn = _rms(h_ref[...], g_ref[...])
    xT_ref[...] = hn.T.astype(BF16)
    q = jnp.dot(hn.astype(BF16), wq_ref[...], preferred_element_type=F32).astype(BF16)
    nt = (((1,), (1,)), ((), ()))
    for h in range(PEER_HEADS):
        c0 = h * 2 * PEER_HALF
        s1 = lax.dot_general(keys_ref[0], q[:, c0:c0 + PEER_HALF], nt, preferred_element_type=F32)
        s2 = lax.dot_general(keys_ref[1], q[:, c0 + PEER_HALF:c0 + 2 * PEER_HALF], nt,
                             preferred_element_type=F32)
        t1 = _top_desc(s1, PEER_TOPK)
        t2, rank2 = _top_desc(s2, PEER_TOPK, with_rank=True)
        cand = jnp.concatenate([t1[0:1] + t2] + [t1[a:a + 1] + t2[0:8] for a in range(1, 8)]
                               + [t1[8:16] + t2[0:1]], axis=0)
        best = _top_desc(cand, PEER_TOPK)
        tau = best[PEER_TOPK - 1:PEER_TOPK]
        z = jnp.sum(jnp.where(cand >= tau, jnp.exp(cand - best[0:1]), 0.0), axis=0, keepdims=True)
        n_sorted = jnp.zeros_like(t1)
        for b in range(PEER_TOPK):
            n_sorted = n_sorted + jnp.where(t1 + t2[b:b + 1] >= tau, 1.0, 0.0)
        n1 = jnp.zeros_like(s1)
        for a in range(PEER_TOPK):
            n1 = jnp.where(s1 == t1[a:a + 1], n_sorted[a:a + 1], n1)
        r2_ref[h] = rank2.astype(BF16)
        e2_ref[h] = jnp.exp(s2 - t2[0:1]).astype(BF16)
        n1_ref[h] = n1
        e1_ref[h] = jnp.exp(s1 - t1[0:1]) / z


def peer_front(h, g, wq, keys, tm=256):
    T, D = h.shape
    tok = lambda: pl.BlockSpec((PEER_HEADS, PEER_NKEYS, tm), lambda i: (0, 0, i))
    tok_shape = jax.ShapeDtypeStruct((PEER_HEADS, PEER_NKEYS, T), F32)
    tok_half = jax.ShapeDtypeStruct((PEER_HEADS, PEER_NKEYS, T), BF16)
    return pl.pallas_call(
        _peer_front_kernel,
        grid=(T // tm,),
        in_specs=[pl.BlockSpec((tm, D), lambda i: (i, 0)),
                  pl.BlockSpec((1, D), lambda i: (0, 0)),
                  pl.BlockSpec(wq.shape, lambda i: (0, 0)),
                  pl.BlockSpec(keys.shape, lambda i: (0, 0, 0))],
        out_specs=[pl.BlockSpec((D, tm), lambda i: (0, i)), tok(), tok(), tok(), tok()],
        out_shape=[jax.ShapeDtypeStruct((D, T), BF16), tok_half, tok_half, tok_shape, tok_shape],
        compiler_params=_cparams("parallel"),
        name="peer_front",
    )(h, g[None, :], wq, keys)


def _transpose_cast_kernel(x_ref, o_ref):
    o_ref[...] = x_ref[...].T.astype(o_ref.dtype)


def transpose_cast(x3, layer, dtype, tr=512):
    _, R, C = x3.shape
    return pl.pallas_call(
        _transpose_cast_kernel,
        grid=(R // tr,),
        in_specs=[pl.BlockSpec((None, tr, C), lambda i: (layer, i, 0))],
        out_specs=pl.BlockSpec((C, tr), lambda i: (0, i)),
        out_shape=jax.ShapeDtypeStruct((C, R), dtype),
        compiler_params=_cparams("parallel"),
        name="transpose_cast",
    )(x3)


PEER_CHUNKS = 4
PEER_HALF_ROWS = PEER_CHUNKS * PEER_NKEYS


def _peer_dense_kernel(xT_ref, u0_ref, ub_ref, ua_ref, vT_ref, r2_ref, e2_ref, n1_ref, e1_ref, h_ref, o_ref,
                       acc_ref, aa_ref, ab_ref):
    e = pl.program_id(1)

    @pl.when(e == 0)
    def _():
        acc_ref[...] = jnp.zeros_like(acc_ref)
        aa_ref[...] = jnp.dot(u0_ref[...], xT_ref[...], preferred_element_type=F32)

    def hidden(a_ref, k0):
        parts = []
        for c in range(PEER_CHUNKS):
            k = k0 + c
            gate = None
            for h in range(PEER_HEADS):
                n1 = n1_ref[h, k:k + 1, :].astype(BF16)
                e1 = e1_ref[h, k:k + 1, :].astype(BF16)
                w = jnp.where(r2_ref[h] < n1, e2_ref[h] * e1, jnp.zeros((), BF16))
                gate = w if gate is None else gate + w
            a = a_ref[c * PEER_NKEYS:(c + 1) * PEER_NKEYS, :]
            gelu = 0.5 * a * (1.0 + lax.erf(a * (2.0 ** -0.5)))
            parts.append(gelu.astype(BF16) * gate)
        return jnp.concatenate(parts, axis=0)

    xT = xT_ref[...]
    ab_ref[...] = jnp.dot(ub_ref[...], xT, preferred_element_type=F32)
    h_a = hidden(aa_ref, 0)
    acc_ref[...] += jnp.dot(vT_ref[:, :PEER_HALF_ROWS], h_a, preferred_element_type=F32)
    aa_ref[...] = jnp.dot(ua_ref[...], xT, preferred_element_type=F32)
    h_b = hidden(ab_ref, PEER_CHUNKS)
    acc_ref[...] += jnp.dot(vT_ref[:, PEER_HALF_ROWS:], h_b, preferred_element_type=F32)

    @pl.when(e == pl.num_programs(1) - 1)
    def _():
        o_ref[...] = h_ref[...] + acc_ref[...].T


def peer_dense(xT, u, vT, r2, e2, n1, e1, h, tm=512):
    D, T = xT.shape
    E = u.shape[0]
    hr = PEER_HALF_ROWS
    ne = E // (2 * hr)
    ni = 2 * PEER_CHUNKS
    tok = lambda: pl.BlockSpec((PEER_HEADS, PEER_NKEYS, tm), lambda i, e: (0, 0, i))
    key = lambda: pl.BlockSpec((PEER_HEADS, ni, tm), lambda i, e: (0, e, i))
    return pl.pallas_call(
        _peer_dense_kernel,
        grid=(T // tm, ne),
        in_specs=[pl.BlockSpec((D, tm), lambda i, e: (0, i)),
                  pl.BlockSpec((hr, D), lambda i, e: (0, 0)),
                  pl.BlockSpec((hr, D), lambda i, e: (2 * e + 1, 0)),
                  pl.BlockSpec((hr, D), lambda i, e: (jnp.minimum(2 * e + 2, 2 * ne - 1), 0)),
                  pl.BlockSpec((D, 2 * hr), lambda i, e: (0, e)),
                  tok(), tok(), key(), key(),
                  pl.BlockSpec((tm, D), lambda i, e: (i, 0))],
        out_specs=pl.BlockSpec((tm, D), lambda i, e: (i, 0)),
        out_shape=jax.ShapeDtypeStruct((T, D), F32),
        scratch_shapes=[pltpu.VMEM((D, tm), F32), pltpu.VMEM((hr, tm), F32), pltpu.VMEM((hr, tm), F32)],
        compiler_params=_cparams("parallel", "arbitrary"),
        name="peer_dense",
    )(xT, u, u, u, vT, r2, e2, n1, e1, h)


def _pad_cols(w, n):
    return jnp.pad(w, ((0, 0), (0, n - w.shape[1])))


def _layer(x, B, S, tabs, layer, w_in_all, peer_v_all, norm_mix, dil_q_norm, dil_k_norm, mla_cq_norm,
           mla_ckv_norm, mla_w_uq, mla_w_ukv, mla_q_norm, mla_k_norm, w_branch_sb, w_branch_dil, w_branch_mla,
           w_out, norm_ffn, peer_w_query, peer_sub_keys, peer_u):
    cos128, sin128, cos64, sin64 = tabs
    c_dil, c_lat = 3 * SB_WIDTH, 3 * SB_WIDTH + 3 * DIL_WIDTH

    w_c = _pad_cols(w_in_all[layer, :, c_lat:MIX_COLS], 896)
    w_g = w_in_all[layer, :, MIX_COLS:]
    p_a, p_b = norm_matmul2(x, norm_mix, w_in_all, layer, c_dil, c_lat - c_dil, BF16, F32, tn=768,
                            name="in_proj_ab")
    p_c = norm_matmul(x, norm_mix, w_c, F32, tn=896, name="in_proj_c")

    o_sb = sb_attention(p_a, B, S)

    o_dil = dil_attention(p_b, dil_q_norm, dil_k_norm, cos128, sin128, B, S)

    uq = mla_w_uq.reshape(MLA_Q_RANK, MLA_HEADS, MLA_QK)
    wqn = uq[:, :, :MLA_NOPE].reshape(MLA_Q_RANK, MLA_HEADS * MLA_NOPE).astype(BF16)
    wqr = jnp.pad(uq[:, :, MLA_NOPE:], ((0, 0), (0, 0), (0, 128 - MLA_ROPE)))
    wqr = wqr.reshape(MLA_Q_RANK, MLA_HEADS * 128).astype(BF16)
    gpad = lambda gn: jnp.pad(gn, (0, MLA_HEAD_PAD - MLA_QK))[None, :]
    q, k, v = mla_prep(p_c, mla_cq_norm[None, :], mla_ckv_norm[None, :], wqn, wqr, mla_w_ukv.astype(BF16),
                       gpad(mla_q_norm * (MLA_QK ** -0.5 * LOG2E)), gpad(mla_k_norm), cos64, sin64)
    o_mla = mla_flash(q, k, v, B, S)

    h = merge_out(x, norm_mix, o_sb, o_dil, o_mla, w_g, w_branch_sb.astype(BF16), w_branch_dil.astype(BF16),
                  w_branch_mla.astype(BF16), w_out.astype(BF16))

    xT, r2, e2, n1, e1 = peer_front(h, norm_ffn, peer_w_query.astype(BF16), peer_sub_keys.astype(BF16))
    return peer_dense(xT, peer_u.astype(BF16), transpose_cast(peer_v_all, layer, BF16), r2, e2, n1, e1, h)


def kernel(x, positions, norm_mix, w_in, dil_q_norm, dil_k_norm, mla_cq_norm, mla_ckv_norm, mla_w_uq, mla_w_ukv,
           mla_q_norm, mla_k_norm, w_branch_sb, w_branch_dil, w_branch_mla, w_out, norm_ffn, peer_w_query,
           peer_sub_keys, peer_u, peer_v):
    B, S, D = x.shape
    pos_col = positions.reshape(B * S, 1)
    tabs = rope_tables(pos_col, HEAD_DIM // 2) + rope_tables(pos_col, MLA_ROPE // 2)
    xf = x.reshape(B * S, D)
    w_in_bf = w_in.astype(BF16)
    for l in range(w_in.shape[0]):
        xf = _layer(xf, B, S, tabs, l, w_in_bf, peer_v, norm_mix[l], dil_q_norm[l], dil_k_norm[l], mla_cq_norm[l],
                    mla_ckv_norm[l], mla_w_uq[l], mla_w_ukv[l], mla_q_norm[l], mla_k_norm[l], w_branch_sb[l],
                    w_branch_dil[l], w_branch_mla[l], w_out[l], norm_ffn[l], peer_w_query[l],
                    peer_sub_keys[l], peer_u[l])
    return xf.reshape(B, S, D)
```

```python
import functools

import numpy as np
import jax
import jax.numpy as jnp
from jax import lax
from jax.experimental import pallas as pl
from jax.experimental.pallas import tpu as pltpu

F32 = jnp.float32
BF16 = jnp.bfloat16

D_MODEL = 2048
HEAD_DIM = 128
ROPE_THETA = 10000.0
EPS = 1e-6
NEG_INF = -1e30
LOG2E = 1.4426950408889634

SB_HEADS = 4
SB_WIDTH = SB_HEADS * HEAD_DIM
DIL_PATTERNS = ((128, 1), (512, 4), (2048, 16))
DIL_HEADS_PER_GROUP = 2
DIL_HEADS = DIL_HEADS_PER_GROUP * len(DIL_PATTERNS)
DIL_WIDTH = DIL_HEADS * HEAD_DIM
DIL_OUT_WIDTH = DIL_HEADS_PER_GROUP * HEAD_DIM
DIL_BLOCK = 128
MLA_HEADS = 6
MLA_NOPE = 128
MLA_ROPE = 64
MLA_V = 128
MLA_QK = MLA_NOPE + MLA_ROPE
MLA_Q_RANK = 512
MLA_KV_RANK = 256
MLA_HEAD_PAD = 256
N_BRANCH = 3
MIX_COLS = 3 * SB_WIDTH + 3 * DIL_WIDTH + MLA_Q_RANK + MLA_KV_RANK + MLA_ROPE
PEER_HEADS = 8
PEER_NKEYS = 128
PEER_EXPERTS = PEER_NKEYS * PEER_NKEYS
PEER_HALF = 128
PEER_TOPK = 16

SB_EXIT = -104.0

FLASH_SLACK = 60.0
FLASH_GROUP = 4

VMEM_LIMIT = 56 * 1024 * 1024


def _cparams(*sem):
    return pltpu.CompilerParams(dimension_semantics=sem, vmem_limit_bytes=VMEM_LIMIT)


def _rms(x, g):
    return x * lax.rsqrt(jnp.mean(x * x, axis=-1, keepdims=True) + EPS) * g


def _rope_table_kernel(pos_ref, inv_ref, sign_ref, cos_ref, sin_ref):
    ang = pos_ref[...].astype(F32) * inv_ref[...]
    cos_ref[...] = jnp.cos(ang)
    sin_ref[...] = jnp.sin(ang) * sign_ref[...]


def rope_tables(pos_col, half, tm=2048):
    T = pos_col.shape[0]
    inv_h = ROPE_THETA ** (-jnp.arange(half, dtype=F32) / half)
    pad = jnp.zeros((128 - 2 * half,), F32)
    inv = jnp.concatenate([inv_h, inv_h, pad])[None, :]
    sign = jnp.concatenate([-jnp.ones((half,), F32), jnp.ones((half,), F32), pad])[None, :]
    row = pl.BlockSpec((1, 128), lambda i: (0, 0))
    cos, sin = pl.pallas_call(
        _rope_table_kernel,
        grid=(T // tm,),
        in_specs=[pl.BlockSpec((tm, 1), lambda i: (i, 0)), row, row],
        out_specs=[pl.BlockSpec((tm, 128), lambda i: (i, 0))] * 2,
        out_shape=[jax.ShapeDtypeStruct((T, 128), F32)] * 2,
        compiler_params=_cparams("parallel"),
        name="rope_tables",
    )(pos_col, inv, sign)
    return cos, sin


def _norm_matmul_kernel(x_ref, g_ref, w_ref, o_ref, u_ref):
    @pl.when(pl.program_id(1) == 0)
    def _():
        u_ref[...] = _rms(x_ref[...], g_ref[...]).astype(BF16)

    o_ref[...] = jnp.dot(u_ref[...], w_ref[...], preferred_element_type=F32).astype(o_ref.dtype)


def norm_matmul(x, g, w, out_dtype, tn, tm=512, name="norm_matmul"):
    T, D = x.shape
    N = w.shape[1]
    return pl.pallas_call(
        _norm_matmul_kernel,
        grid=(T // tm, N // tn),
        in_specs=[pl.BlockSpec((tm, D), lambda i, j: (i, 0)),
                  pl.BlockSpec((1, D), lambda i, j: (0, 0)),
                  pl.BlockSpec((D, tn), lambda i, j: (0, j))],
        out_specs=pl.BlockSpec((tm, tn), lambda i, j: (i, j)),
        out_shape=jax.ShapeDtypeStruct((T, N), out_dtype),
        scratch_shapes=[pltpu.VMEM((tm, D), BF16)],
        compiler_params=_cparams("parallel", "arbitrary"),
        name=name,
    )(x, g[None, :], w)


def _norm_matmul2_kernel(x_ref, g_ref, w_ref, oa_ref, ob_ref, u_ref, *, na):
    j = pl.program_id(1)

    @pl.when(j == 0)
    def _():
        u_ref[...] = _rms(x_ref[...], g_ref[...]).astype(BF16)

    res = jnp.dot(u_ref[...], w_ref[...], preferred_element_type=F32)

    @pl.when(j < na)
    def _():
        oa_ref[...] = res.astype(oa_ref.dtype)

    @pl.when(j >= na)
    def _():
        ob_ref[...] = res.astype(ob_ref.dtype)


def norm_matmul2(x, g, w3, layer, n_a, n_b, dtype_a, dtype_b, tn, tm=512, name="norm_matmul2"):
    T, D = x.shape
    na, nb = n_a // tn, n_b // tn
    kern = functools.partial(_norm_matmul2_kernel, na=na)
    return pl.pallas_call(
        kern,
        grid=(T // tm, na + nb),
        in_specs=[pl.BlockSpec((tm, D), lambda i, j: (i, 0)),
                  pl.BlockSpec((1, D), lambda i, j: (0, 0)),
                  pl.BlockSpec((None, D, tn), lambda i, j: (layer, 0, j))],
        out_specs=[pl.BlockSpec((tm, tn), lambda i, j: (i, jnp.minimum(j, na - 1))),
                   pl.BlockSpec((tm, tn), lambda i, j: (i, jnp.maximum(j - na, 0)))],
        out_shape=[jax.ShapeDtypeStruct((T, n_a), dtype_a), jax.ShapeDtypeStruct((T, n_b), dtype_b)],
        scratch_shapes=[pltpu.VMEM((tm, D), BF16)],
        compiler_params=_cparams("parallel", "arbitrary"),
        name=name,
    )(x, g[None, :], w3)


def _sb_kernel(q_ref, k_ref, v_ref, tri_ref, o_ref, acc_ref, carry_ref, *, tq, tk, scale):
    q0 = pl.program_id(1) * tq
    acc_ref[...] = jnp.zeros_like(acc_ref)
    carry_ref[...] = jnp.zeros_like(carry_ref)
    tri = tri_ref[...]
    row = q0 + lax.broadcasted_iota(jnp.int32, (tq, tk), 0)
    lane = lax.broadcasted_iota(jnp.int32, (tq, tk), 1)

    def cond(c):
        kb, top = c
        return jnp.logical_and(kb >= 0, top > SB_EXIT)

    def body(c):
        kb, _ = c
        ks = pl.multiple_of(kb * tk, tk)
        mask = (ks + lane) < row
        tops = []
        for h in range(SB_HEADS):
            sl = slice(h * HEAD_DIM, (h + 1) * HEAD_DIM)
            k = k_ref[pl.ds(ks, tk), sl]
            v = v_ref[pl.ds(ks, tk), sl]
            z = lax.dot_general(q_ref[:, sl], k, (((1,), (1,)), ((), ())), preferred_element_type=F32) * scale
            lk = -(jnp.maximum(z, 0.0) + jnp.log1p(jnp.exp(-jnp.abs(z))))
            lk = jnp.where(mask, lk, 0.0)
            hi = lk.astype(BF16)
            lo = (lk - hi.astype(F32)).astype(BF16)
            loc = (jnp.dot(hi, tri, preferred_element_type=F32)
                   + jnp.dot(lo, tri, preferred_element_type=F32))
            carry = carry_ref[:, sl]
            after = carry + loc[:, :tk]
            w = jnp.where(mask, jnp.exp(z + lk + after), 0.0)
            acc_ref[:, sl] += jnp.dot(w.astype(BF16), v, preferred_element_type=F32)
            new_carry = carry + loc[:, tk:]
            carry_ref[:, sl] = new_carry
            tops.append(jnp.max(new_carry))
        return kb - 1, functools.reduce(jnp.maximum, tops)

    lax.while_loop(cond, body, ((q0 + tq) // tk - 1, jnp.float32(0.0)))
    o_ref[...] = acc_ref[...].astype(o_ref.dtype)


def sb_attention(qkv, B, S, tq=256, tk=HEAD_DIM):
    T = B * S
    nq = S // tq
    tri = np.concatenate([np.triu(np.ones((tk, tk), np.float32), 1).T, np.ones((tk, tk), np.float32)], axis=1)
    tri = jnp.asarray(tri, BF16)
    kern = functools.partial(_sb_kernel, tq=tq, tk=tk, scale=HEAD_DIM ** -0.5)
    return pl.pallas_call(
        kern,
        grid=(B, nq),
        in_specs=[pl.BlockSpec((tq, SB_WIDTH), lambda b, i: (b * nq + i, 0)),
                  pl.BlockSpec((S, SB_WIDTH), lambda b, i: (b, 1)),
                  pl.BlockSpec((S, SB_WIDTH), lambda b, i: (b, 2)),
                  pl.BlockSpec((tk, 2 * tk), lambda b, i: (0, 0))],
        out_specs=pl.BlockSpec((tq, SB_WIDTH), lambda b, i: (b * nq + i, 0)),
        out_shape=jax.ShapeDtypeStruct((T, SB_WIDTH), BF16),
        scratch_shapes=[pltpu.VMEM((tq, SB_WIDTH), F32), pltpu.VMEM((tq, SB_WIDTH), F32)],
        compiler_params=_cparams("parallel", "arbitrary"),
        name="sb_attention",
    )(qkv, qkv, qkv, tri)


DIL_CHUNK = 2048


def _dil_group(g, d, chunk, q_refs, k_refs, v_refs, cos_ref, sin_ref, gq, gk, kcache_ref, vcache_ref,
               oacc_ref, lse_ref):
    first_chunk = chunk == 0
    par = chunk % 2
    blk = DIL_BLOCK
    nblk = DIL_CHUNK // (blk * d)
    n_back = DIL_PATTERNS[g][0] // d
    scale = HEAD_DIM ** -0.5
    qi = lax.broadcasted_iota(jnp.int32, (blk, 2 * blk), 0)
    kj = lax.broadcasted_iota(jnp.int32, (blk, 2 * blk), 1)
    dist = blk + qi - kj
    band = jnp.logical_and(dist >= 0, dist <= n_back)
    band_first = jnp.logical_and(band, jnp.logical_or(kj >= blk, jnp.logical_not(first_chunk)))
    nt = (((1,), (1,)), ((), ()))
    ones = jnp.ones((2 * blk, HEAD_DIM), BF16)

    @pl.when(first_chunk)
    def _():
        kcache_ref[1, g] = jnp.zeros(kcache_ref.shape[2:], BF16)
        vcache_ref[1, g] = jnp.zeros(vcache_ref.shape[2:], BF16)

    def rows(m, r):
        return pl.ds(m * blk * d + r, blk, stride=d) if d > 1 else pl.ds(m * blk, blk)

    def prep(x, gain, cos, sin):
        y = _rms(x, gain)
        return (y * cos + pltpu.roll(y, HEAD_DIM // 2, 1) * sin).astype(BF16)

    for m in range(nblk):
        for r in range(d):
            cur = rows(m, r)
            cos_c, sin_c = cos_ref[cur, :], sin_ref[cur, :]
            slot = pl.ds((m * d + r) * blk, blk)
            prev_half = par if m > 0 else 1 - par
            prev_slot = pl.ds(((m - 1 if m > 0 else nblk - 1) * d + r) * blk, blk)
            mask = band if m > 0 else band_first
            for h in range(DIL_HEADS_PER_GROUP):
                k_prev = kcache_ref[prev_half, g, h, prev_slot, :]
                v_prev = vcache_ref[prev_half, g, h, prev_slot, :]
                q = prep(q_refs[h][cur, :], gq, cos_c, sin_c)
                k_cur = prep(k_refs[h][cur, :], gk, cos_c, sin_c)
                v_cur = v_refs[h][cur, :].astype(BF16)
                kcache_ref[par, g, h, slot, :] = k_cur
                vcache_ref[par, g, h, slot, :] = v_cur
                keys = jnp.concatenate([k_prev, k_cur], axis=0)
                vals = jnp.concatenate([jnp.concatenate([v_prev, v_cur], axis=0), ones], axis=1)
                s = lax.dot_general(q, keys, nt, preferred_element_type=F32) * scale
                s = jnp.where(mask, s, NEG_INF)
                mx = jnp.max(s, axis=-1, keepdims=True)
                acc = jnp.dot(jnp.exp(s - mx).astype(BF16), vals, preferred_element_type=F32)
                den = acc[:, HEAD_DIM:]
                oacc_ref[g, h, cur, :] = acc[:, :HEAD_DIM] / den
                lse_ref[g, h, cur, :] = mx + jnp.log(den)


def _dil_kernel(*refs):
    nh = DIL_HEADS_PER_GROUP
    q_refs, k_refs, v_refs = (refs[i * nh:(i + 1) * nh] for i in range(3))
    cos_ref, sin_ref, gq_ref, gk_ref, o_ref, kcache_ref, vcache_ref, oacc_ref, lse_ref = refs[3 * nh:]
    chunk = pl.program_id(1)
    g = pl.program_id(2)
    gq, gk = gq_ref[...], gk_ref[...]
    for gi, (_, d) in enumerate(DIL_PATTERNS):
        @pl.when(g == gi)
        def _(gi=gi, d=d):
            _dil_group(gi, d, chunk, q_refs, k_refs, v_refs, cos_ref, sin_ref, gq, gk, kcache_ref,
                       vcache_ref, oacc_ref, lse_ref)

    @pl.when(g == len(DIL_PATTERNS) - 1)
    def _():
        for h in range(nh):
            a0, a1, a2 = lse_ref[0, h], lse_ref[1, h], lse_ref[2, h]
            mx = jnp.maximum(jnp.maximum(a0, a1), a2)
            e0, e1, e2 = jnp.exp(a0 - mx), jnp.exp(a1 - mx), jnp.exp(a2 - mx)
            den = e0 + e1 + e2
            o_ref[:, h * HEAD_DIM:(h + 1) * HEAD_DIM] = (
                (e0 / den) * oacc_ref[0, h] + (e1 / den) * oacc_ref[1, h] + (e2 / den) * oacc_ref[2, h]
            ).astype(o_ref.dtype)


def dil_attention(p, gq, gk, cos, sin, B, S):
    T = B * S
    nch = S // DIL_CHUNK
    ng, nh = len(DIL_PATTERNS), DIL_HEADS_PER_GROUP

    def heads(part):
        return [pl.BlockSpec((DIL_CHUNK, HEAD_DIM),
                             lambda b, n, g, h=h: (b * nch + n, part * DIL_HEADS + g * nh + h)) for h in range(nh)]

    tab = pl.BlockSpec((DIL_CHUNK, HEAD_DIM), lambda b, n, g: (b * nch + n, 0))
    gain = pl.BlockSpec((1, HEAD_DIM), lambda b, n, g: (0, 0))
    return pl.pallas_call(
        _dil_kernel,
        grid=(B, nch, ng),
        in_specs=heads(0) + heads(1) + heads(2) + [tab, tab, gain, gain],
        out_specs=pl.BlockSpec((DIL_CHUNK, DIL_OUT_WIDTH), lambda b, n, g: (b * nch + n, 0)),
        out_shape=jax.ShapeDtypeStruct((T, DIL_OUT_WIDTH), BF16),
        scratch_shapes=[pltpu.VMEM((2, ng, nh, DIL_CHUNK, HEAD_DIM), BF16),
                        pltpu.VMEM((2, ng, nh, DIL_CHUNK, HEAD_DIM), BF16),
                        pltpu.VMEM((ng, nh, DIL_CHUNK, HEAD_DIM), F32),
                        pltpu.VMEM((ng, nh, DIL_CHUNK, HEAD_DIM), F32)],
        compiler_params=_cparams("arbitrary", "arbitrary", "arbitrary"),
        name="dil_attention",
    )(*([p] * (3 * nh)), cos, sin, gq[None, :], gk[None, :])


def _mla_prep_kernel(lat_ref, gcq_ref, gckv_ref, wqn_ref, wqr_ref, wkv_ref, gq_ref, gk_ref,
                     cos_ref, sin_ref, q_ref, k_ref, v_ref):
    lat = lat_ref[...]
    cq = _rms(lat[:, :MLA_Q_RANK], gcq_ref[...]).astype(BF16)
    ckv = _rms(lat[:, MLA_Q_RANK:MLA_Q_RANK + MLA_KV_RANK], gckv_ref[...]).astype(BF16)
    kpe = lat[:, MLA_Q_RANK + MLA_KV_RANK:]
    qn = jnp.dot(cq, wqn_ref[...], preferred_element_type=F32)
    qr = jnp.dot(cq, wqr_ref[...], preferred_element_type=F32)
    kv = jnp.dot(ckv, wkv_ref[...], preferred_element_type=F32)
    cos, sin = cos_ref[...], sin_ref[...]
    gq, gk = gq_ref[...], gk_ref[...]
    kpe_ss = jnp.sum(kpe * kpe, axis=-1, keepdims=True)

    def rope64(y):
        return y * cos + (pltpu.roll(y, 32, 1) + pltpu.roll(y, 96, 1)) * sin

    for h in range(MLA_HEADS):
        a = slice(h * 128, (h + 1) * 128)
        n0 = h * MLA_HEAD_PAD
        qn_h, qr_h = qn[:, a], qr[:, a]
        ms = (jnp.sum(qn_h * qn_h, axis=-1, keepdims=True)
              + jnp.sum(qr_h * qr_h, axis=-1, keepdims=True)) * (1.0 / MLA_QK)
        r = lax.rsqrt(ms + EPS)
        q_ref[:, n0:n0 + 128] = (qn_h * r * gq[:, :128]).astype(BF16)
        q_ref[:, n0 + 128:n0 + 256] = rope64(qr_h * r * gq[:, 128:]).astype(BF16)
        kn_h = kv[:, n0:n0 + 128]
        ms = (jnp.sum(kn_h * kn_h, axis=-1, keepdims=True) + kpe_ss) * (1.0 / MLA_QK)
        r = lax.rsqrt(ms + EPS)
        k_ref[:, n0:n0 + 128] = (kn_h * r * gk[:, :128]).astype(BF16)
        k_ref[:, n0 + 128:n0 + 256] = rope64(kpe * r * gk[:, 128:]).astype(BF16)
        v_ref[:, n0:n0 + 128] = kv[:, n0 + 128:n0 + 256].astype(BF16)
        v_ref[:, n0 + 128:n0 + 256] = jnp.ones((kv.shape[0], 128), BF16)


def mla_prep(lat, gcq, gckv, wqn, wqr, wkv, gq, gk, cos, sin, tm=512):
    T = lat.shape[0]
    full = lambda a: pl.BlockSpec(a.shape, lambda i: (0,) * a.ndim)
    rows = lambda w: pl.BlockSpec((tm, w), lambda i: (i, 0))
    return pl.pallas_call(
        _mla_prep_kernel,
        grid=(T // tm,),
        in_specs=[rows(lat.shape[1]), full(gcq), full(gckv), full(wqn), full(wqr), full(wkv),
                  full(gq), full(gk), rows(128), rows(128)],
        out_specs=[rows(MLA_HEADS * MLA_HEAD_PAD)] * 3,
        out_shape=[jax.ShapeDtypeStruct((T, MLA_HEADS * MLA_HEAD_PAD), BF16)] * 3,
        compiler_params=_cparams("parallel"),
        name="mla_prep",
    )(lat, gcq, gckv, wqn, wqr, wkv, gq, gk, cos, sin)


def _flash_kernel(q_ref, k_ref, v_ref, o_ref, acc_ref, m_ref, *, t):
    qi = pl.program_id(2)
    q = q_ref[...]
    nt = (((1,), (1,)), ((), ()))
    chunks = [slice(c * 128, (c + 1) * 128) for c in range(t // 128)]

    def probs(s, m):
        return jnp.concatenate([jnp.exp2(s[:, c] - m) for c in chunks], axis=1).astype(BF16)

    kd = pl.multiple_of(qi * t, t)
    s = lax.dot_general(q, k_ref[pl.ds(kd, t), :], nt, preferred_element_type=F32)
    r = lax.broadcasted_iota(jnp.int32, (t, t), 0)
    c = lax.broadcasted_iota(jnp.int32, (t, t), 1)
    s = jnp.where(c <= r, s, NEG_INF)
    m0 = jnp.broadcast_to(jnp.max(s, axis=-1, keepdims=True), (t, 128))
    m_ref[...] = m0
    acc_ref[...] = jnp.dot(probs(s, m0), v_ref[pl.ds(kd, t), :], preferred_element_type=F32)

    def rescaled(ks):
        s2 = lax.dot_general(q_ref[...], k_ref[pl.ds(ks, t), :], nt, preferred_element_type=F32)
        m_old = m_ref[...]
        m_new = jnp.maximum(m_old, jnp.max(s2, axis=-1, keepdims=True))
        alpha = jnp.exp2(m_old - m_new)
        pv2 = jnp.dot(probs(s2, m_new), v_ref[pl.ds(ks, t), :], preferred_element_type=F32)
        acc_ref[...] = jnp.concatenate([alpha, alpha], axis=1) * acc_ref[...] + pv2
        m_ref[...] = m_new

    def blocks(starts):
        m = m_ref[...]
        over, pv = None, None
        for ks in starts:
            s = lax.dot_general(q, k_ref[pl.ds(ks, t), :], nt, preferred_element_type=F32)
            d = [s[:, c] - m for c in chunks]
            over = functools.reduce(jnp.maximum, d if over is None else d + [over])
            p = jnp.concatenate([jnp.exp2(dc) for dc in d], axis=1).astype(BF16)
            part = jnp.dot(p, v_ref[pl.ds(ks, t), :], preferred_element_type=F32)
            pv = part if pv is None else pv + part
        ok = jnp.max(over) <= FLASH_SLACK

        @pl.when(ok)
        def _():
            acc_ref[...] += pv

        @pl.when(jnp.logical_not(ok))
        def _():
            for ks in starts:
                rescaled(ks)

    def group(kg, carry):
        ks = pl.multiple_of(kg * (FLASH_GROUP * t), FLASH_GROUP * t)
        blocks([pl.multiple_of(ks + i * t, t) for i in range(FLASH_GROUP)])
        return carry

    def single(kb, carry):
        blocks([pl.multiple_of(kb * t, t)])
        return carry

    n_grouped = (qi // FLASH_GROUP) * FLASH_GROUP
    lax.fori_loop(0, qi // FLASH_GROUP, group, 0)
    lax.fori_loop(n_grouped, qi, single, 0)

    o_ref[...] = (acc_ref[:, :MLA_V] / acc_ref[:, MLA_V:]).astype(o_ref.dtype)


def mla_flash(q, k, v1, B, S, t=512):
    T = B * S
    nq = S // t
    kern = functools.partial(_flash_kernel, t=t)
    return pl.pallas_call(
        kern,
        grid=(B, MLA_HEADS, nq),
        in_specs=[pl.BlockSpec((t, MLA_HEAD_PAD), lambda b, h, i: (b * nq + i, h)),
                  pl.BlockSpec((S, MLA_HEAD_PAD), lambda b, h, i: (b, h)),
                  pl.BlockSpec((S, 2 * MLA_V), lambda b, h, i: (b, h))],
        out_specs=pl.BlockSpec((t, MLA_V), lambda b, h, i: (b * nq + i, h)),
        out_shape=jax.ShapeDtypeStruct((T, MLA_HEADS * MLA_V), BF16),
        scratch_shapes=[pltpu.VMEM((t, 2 * MLA_V), F32), pltpu.VMEM((t, 128), F32)],
        compiler_params=_cparams("parallel", "parallel", "arbitrary"),
        name="mla_flash",
    )(q, k, v1)


def _merge_kernel(x_ref, g_ref, osb_ref, odil_ref, omla_ref, wg0_ref, wg1_ref, wg2_ref,
                  wsb_ref, wdil_ref, wmla_ref, wout_ref, o_ref, u_ref):
    j = pl.program_id(1)

    @pl.when(j == 0)
    def _():
        x = x_ref[...]
        u_ref[...] = _rms(x, g_ref[...]).astype(BF16)
        o_ref[...] = x

    u = u_ref[...]

    def branch(o_b, wg, wb):
        gate = jax.nn.sigmoid(jnp.dot(u, wg[...], preferred_element_type=F32))
        return gate * jnp.dot(o_b[...], wb[...], preferred_element_type=F32)

    merged = (branch(osb_ref, wg0_ref, wsb_ref) + branch(odil_ref, wg1_ref, wdil_ref)
              + branch(omla_ref, wg2_ref, wmla_ref))
    o_ref[...] += jnp.dot(merged.astype(BF16), wout_ref[...], preferred_element_type=F32)


def merge_out(x, g, osb, odil, omla, wg, wsb, wdil, wmla, wout, tm=512, tn=512):
    T, D = x.shape
    nj = D // tn
    rows = lambda w: pl.BlockSpec((tm, w), lambda i, j: (i, 0))
    cols = lambda k: pl.BlockSpec((k, tn), lambda i, j: (0, j))
    gcol = lambda b: pl.BlockSpec((D, tn), lambda i, j: (0, b * nj + j))
    return pl.pallas_call(
        _merge_kernel,
        grid=(T // tm, nj),
        in_specs=[rows(D), pl.BlockSpec((1, D), lambda i, j: (0, 0)),
                  rows(osb.shape[1]), rows(odil.shape[1]), rows(omla.shape[1]),
                  gcol(0), gcol(1), gcol(2),
                  cols(wsb.shape[0]), cols(wdil.shape[0]), cols(wmla.shape[0]),
                  pl.BlockSpec((tn, D), lambda i, j: (j, 0))],
        out_specs=rows(D),
        out_shape=jax.ShapeDtypeStruct((T, D), F32),
        scratch_shapes=[pltpu.VMEM((tm, D), BF16)],
        compiler_params=_cparams("parallel", "arbitrary"),
        name="merge_out",
    )(x, g[None, :], osb, odil, omla, wg, wg, wg, wsb, wdil, wmla, wout)


def _top_desc(s, k, with_rank=False):
    vals = []
    rank = jnp.full_like(s, float(k))
    for r in range(k):
        mx = jnp.max(s, axis=0, keepdims=True)
        vals.append(mx)
        hit = s == mx
        if with_rank:
            rank = jnp.where(hit, float(r), rank)
        s = jnp.where(hit, -jnp.inf, s)
    vals = jnp.concatenate(vals, axis=0)
    return (vals, rank) if with_rank else vals


def _peer_front_kernel(h_ref, g_ref, wq_ref, keys_ref, xT_ref, r2_ref, e2_ref, n1_ref, e1_ref):
    hn = _rms(h_ref[...], g_ref[...])
    xT_ref[...] = hn.T.astype(BF16)
    q = jnp.dot(hn.astype(BF16), wq_ref[...], preferred_element_type=F32).astype(BF16)
    nt = (((1,), (1,)), ((), ()))
    for h in range(PEER_HEADS):
        c0 = h * 2 * PEER_HALF
        s1 = lax.dot_general(keys_ref[0], q[:, c0:c0 + PEER_HALF], nt, preferred_element_type=F32)
        s2 = lax.dot_general(keys_ref[1], q[:, c0 + PEER_HALF:c0 + 2 * PEER_HALF], nt,
                             preferred_element_type=F32)
        t1 = _top_desc(s1, PEER_TOPK)
        t2, rank2 = _top_desc(s2, PEER_TOPK, with_rank=True)
        cand = jnp.concatenate([t1[0:1] + t2] + [t1[a:a + 1] + t2[0:8] for a in range(1, 8)]
                               + [t1[8:16] + t2[0:1]], axis=0)
        best = _top_desc(cand, PEER_TOPK)
        tau = best[PEER_TOPK - 1:PEER_TOPK]
        z = jnp.sum(jnp.where(cand >= tau, jnp.exp(cand - best[0:1]), 0.0), axis=0, keepdims=True)
        n_sorted = jnp.zeros_like(t1)
        for b in range(PEER_TOPK):
            n_sorted = n_sorted + jnp.where(t1 + t2[b:b + 1] >= tau, 1.0, 0.0)
        n1 = jnp.zeros_like(s1)
        for a in range(PEER_TOPK):
            n1 = jnp.where(s1 == t1[a:a + 1], n_sorted[a:a + 1], n1)
        r2_ref[h] = rank2.astype(BF16)
        e2_ref[h] = jnp.exp(s2 - t2[0:1]).astype(BF16)
        n1_ref[h] = n1
        e1_ref[h] = jnp.exp(s1 - t1[0:1]) / z


def peer_front(h, g, wq, keys, tm=256):
    T, D = h.shape
    tok = lambda: pl.BlockSpec((PEER_HEADS, PEER_NKEYS, tm), lambda i: (0, 0, i))
    tok_shape = jax.ShapeDtypeStruct((PEER_HEADS, PEER_NKEYS, T), F32)
    tok_half = jax.ShapeDtypeStruct((PEER_HEADS, PEER_NKEYS, T), BF16)
    return pl.pallas_call(
        _peer_front_kernel,
        grid=(T // tm,),
        in_specs=[pl.BlockSpec((tm, D), lambda i: (i, 0)),
                  pl.BlockSpec((1, D), lambda i: (0, 0)),
                  pl.BlockSpec(wq.shape, lambda i: (0, 0)),
                  pl.BlockSpec(keys.shape, lambda i: (0, 0, 0))],
        out_specs=[pl.BlockSpec((D, tm), lambda i: (0, i)), tok(), tok(), tok(), tok()],
        out_shape=[jax.ShapeDtypeStruct((D, T), BF16), tok_half, tok_half, tok_shape, tok_shape],
        compiler_params=_cparams("parallel"),
        name="peer_front",
    )(h, g[None, :], wq, keys)


def _transpose_cast_kernel(x_ref, o_ref):
    o_ref[...] = x_ref[...].T.astype(o_ref.dtype)


def transpose_cast(x3, layer, dtype, tr=512):
    _, R, C = x3.shape
    return pl.pallas_call(
        _transpose_cast_kernel,
        grid=(R // tr,),
        in_specs=[pl.BlockSpec((None, tr, C), lambda i: (layer, i, 0))],
        out_specs=pl.BlockSpec((C, tr), lambda i: (0, i)),
        out_shape=jax.ShapeDtypeStruct((C, R), dtype),
        compiler_params=_cparams("parallel"),
        name="transpose_cast",
    )(x3)


PEER_CHUNKS = 4
PEER_HALF_ROWS = PEER_CHUNKS * PEER_NKEYS


def _peer_dense_kernel(xT_ref, u0_ref, ub_ref, ua_ref, vT_ref, r2_ref, e2_ref, n1_ref, e1_ref, h_ref, o_ref,
                       acc_ref, aa_ref, ab_ref):
    e = pl.program_id(1)

    @pl.when(e == 0)
    def _():
        acc_ref[...] = jnp.zeros_like(acc_ref)
        aa_ref[...] = jnp.dot(u0_ref[...], xT_ref[...], preferred_element_type=F32)

    def hidden(a_ref, k0):
        parts = []
        for c in range(PEER_CHUNKS):
            k = k0 + c
            gate = None
            for h in range(PEER_HEADS):
                n1 = n1_ref[h, k:k + 1, :].astype(BF16)
                e1 = e1_ref[h, k:k + 1, :].astype(BF16)
                w = jnp.where(r2_ref[h] < n1, e2_ref[h] * e1, jnp.zeros((), BF16))
                gate = w if gate is None else gate + w
            a = a_ref[c * PEER_NKEYS:(c + 1) * PEER_NKEYS, :]
            gelu = 0.5 * a * (1.0 + lax.erf(a * (2.0 ** -0.5)))
            parts.append(gelu.astype(BF16) * gate)
        return jnp.concatenate(parts, axis=0)

    xT = xT_ref[...]
    ab_ref[...] = jnp.dot(ub_ref[...], xT, preferred_element_type=F32)
    h_a = hidden(aa_ref, 0)
    acc_ref[...] += jnp.dot(vT_ref[:, :PEER_HALF_ROWS], h_a, preferred_element_type=F32)
    aa_ref[...] = jnp.dot(ua_ref[...], xT, preferred_element_type=F32)
    h_b = hidden(ab_ref, PEER_CHUNKS)
    acc_ref[...] += jnp.dot(vT_ref[:, PEER_HALF_ROWS:], h_b, preferred_element_type=F32)

    @pl.when(e == pl.num_programs(1) - 1)
    def _():
        o_ref[...] = h_ref[...] + acc_ref[...].T


def peer_dense(xT, u3, layer, vT, r2, e2, n1, e1, h, tm=512):
    D, T = xT.shape
    E = u3.shape[1]
    hr = PEER_HALF_ROWS
    ne = E // (2 * hr)
    ni = 2 * PEER_CHUNKS
    tok = lambda: pl.BlockSpec((PEER_HEADS, PEER_NKEYS, tm), lambda i, e: (0, 0, i))
    key = lambda: pl.BlockSpec((PEER_HEADS, ni, tm), lambda i, e: (0, e, i))
    return pl.pallas_call(
        _peer_dense_kernel,
        grid=(T // tm, ne),
        in_specs=[pl.BlockSpec((D, tm), lambda i, e: (0, i)),
                  pl.BlockSpec((None, hr, D), lambda i, e: (layer, 0, 0)),
                  pl.BlockSpec((None, hr, D), lambda i, e: (layer, 2 * e + 1, 0)),
                  pl.BlockSpec((None, hr, D), lambda i, e: (layer, jnp.minimum(2 * e + 2, 2 * ne - 1), 0)),
                  pl.BlockSpec((D, 2 * hr), lambda i, e: (0, e)),
                  tok(), tok(), key(), key(),
                  pl.BlockSpec((tm, D), lambda i, e: (i, 0))],
        out_specs=pl.BlockSpec((tm, D), lambda i, e: (i, 0)),
        out_shape=jax.ShapeDtypeStruct((T, D), F32),
        scratch_shapes=[pltpu.VMEM((D, tm), F32), pltpu.VMEM((hr, tm), F32), pltpu.VMEM((hr, tm), F32)],
        compiler_params=_cparams("parallel", "arbitrary"),
        name="peer_dense",
    )(xT, u3, u3, u3, vT, r2, e2, n1, e1, h)


def _pad_cols(w, n):
    return jnp.pad(w, ((0, 0), (0, n - w.shape[1])))


def _layer(x, B, S, tabs, layer, w_in_all, peer_v_all, norm_mix, dil_q_norm, dil_k_norm, mla_cq_norm,
           mla_ckv_norm, mla_w_uq, mla_w_ukv, mla_q_norm, mla_k_norm, w_branch_sb, w_branch_dil, w_branch_mla,
           w_out, norm_ffn, peer_w_query, peer_sub_keys, peer_u_all):
    cos128, sin128, cos64, sin64 = tabs
    c_dil, c_lat = 3 * SB_WIDTH, 3 * SB_WIDTH + 3 * DIL_WIDTH

    w_c = _pad_cols(w_in_all[layer, :, c_lat:MIX_COLS], 896)
    w_g = w_in_all[layer, :, MIX_COLS:]
    p_a, p_b = norm_matmul2(x, norm_mix, w_in_all, layer, c_dil, c_lat - c_dil, BF16, F32, tn=768,
                            name="in_proj_ab")
    p_c = norm_matmul(x, norm_mix, w_c, F32, tn=896, name="in_proj_c")

    o_sb = sb_attention(p_a, B, S)

    o_dil = dil_attention(p_b, dil_q_norm, dil_k_norm, cos128, sin128, B, S)

    uq = mla_w_uq.reshape(MLA_Q_RANK, MLA_HEADS, MLA_QK)
    wqn = uq[:, :, :MLA_NOPE].reshape(MLA_Q_RANK, MLA_HEADS * MLA_NOPE).astype(BF16)
    wqr = jnp.pad(uq[:, :, MLA_NOPE:], ((0, 0), (0, 0), (0, 128 - MLA_ROPE)))
    wqr = wqr.reshape(MLA_Q_RANK, MLA_HEADS * 128).astype(BF16)
    gpad = lambda gn: jnp.pad(gn, (0, MLA_HEAD_PAD - MLA_QK))[None, :]
    q, k, v = mla_prep(p_c, mla_cq_norm[None, :], mla_ckv_norm[None, :], wqn, wqr, mla_w_ukv.astype(BF16),
                       gpad(mla_q_norm * (MLA_QK ** -0.5 * LOG2E)), gpad(mla_k_norm), cos64, sin64)
    o_mla = mla_flash(q, k, v, B, S)

    h = merge_out(x, norm_mix, o_sb, o_dil, o_mla, w_g, w_branch_sb.astype(BF16), w_branch_dil.astype(BF16),
                  w_branch_mla.astype(BF16), w_out.astype(BF16))

    xT, r2, e2, n1, e1 = peer_front(h, norm_ffn, peer_w_query.astype(BF16), peer_sub_keys.astype(BF16))
    return peer_dense(xT, peer_u_all, layer, transpose_cast(peer_v_all, layer, BF16), r2, e2, n1, e1, h)


def kernel(x, positions, norm_mix, w_in, dil_q_norm, dil_k_norm, mla_cq_norm, mla_ckv_norm, mla_w_uq, mla_w_ukv,
           mla_q_norm, mla_k_norm, w_branch_sb, w_branch_dil, w_branch_mla, w_out, norm_ffn, peer_w_query,
           peer_sub_keys, peer_u, peer_v):
    B, S, D = x.shape
    pos_col = positions.reshape(B * S, 1)
    tabs = rope_tables(pos_col, HEAD_DIM // 2) + rope_tables(pos_col, MLA_ROPE // 2)
    xf = x.reshape(B * S, D)
    w_in_bf = w_in.astype(BF16)
    peer_u_bf = peer_u.astype(BF16)
    for l in range(w_in.shape[0]):
        xf = _layer(xf, B, S, tabs, l, w_in_bf, peer_v, norm_mix[l], dil_q_norm[l], dil_k_norm[l], mla_cq_norm[l],
                    mla_ckv_norm[l], mla_w_uq[l], mla_w_ukv[l], mla_q_norm[l], mla_k_norm[l], w_branch_sb[l],
                    w_branch_dil[l], w_branch_mla[l], w_out[l], norm_ffn[l], peer_w_query[l],
                    peer_sub_keys[l], peer_u_bf)
    return xf.reshape(B, S, D)
```
